```python
import math
import jax, jax.numpy as jnp
from jax import lax
import numpy as np

D_MODEL = 1024
BATCH = 16
SEQ = 4096
DEPTH = 2

HEAD_DIM = 64
N_HEADS_A = 8
N_HEADS_B = 8
W_A = N_HEADS_A * HEAD_DIM
W_B = N_HEADS_B * HEAD_DIM
DILATED_PAIRS = ((128, 1), (512, 4), (2048, 16))
DSW_BLOCK = 64
GRID_W = 64
NA_ROWS = 8
NA_COLS = 16
NA_COL_BLOCK = 16
NA_COL_REGION = NA_COL_BLOCK + NA_COLS
N_HEADS_C = D_MODEL // (2 * HEAD_DIM)
DIFF_BLOCK = 128
D_FF = -(-8 * D_MODEL // (3 * 256)) * 256
RMS_EPS = 1e-6
SUBLN_EPS = 1e-5

kernel_name = "hybrid_dilated_natten_diffattn_encoder"


def rmsnorm(x, g, eps=RMS_EPS):
    xf = x.astype(jnp.float32)
    y = xf * lax.rsqrt(jnp.mean(xf * xf, axis=-1, keepdims=True) + eps)
    return (y * g.astype(jnp.float32)).astype(x.dtype)


def alibi_slopes(n):
    return jnp.exp2(-8.0 * jnp.arange(1, n + 1, dtype=jnp.float32) / n)


def lambda_init_fn(layer):
    return 0.8 - 0.6 * math.exp(-0.3 * layer)


def swiglu(h, w_gate, w_up, w_down):
    return (jax.nn.silu(h @ w_gate) * (h @ w_up)) @ w_down


def dilated_window_branch(q, k, v, slopes, window, dilation):
    B, H, T, Dh = q.shape
    half = window // (2 * dilation)
    L = T // dilation
    nb = -(-L // DSW_BLOCK)
    Lq = nb * DSW_BLOCK
    K = DSW_BLOCK + 2 * half

    def strided(a):
        return a.reshape(B, H, L, dilation, Dh).transpose(0, 1, 3, 2, 4)

    qs = jnp.pad(strided(q), ((0, 0), (0, 0), (0, 0), (0, Lq - L), (0, 0)))
    kpad = ((0, 0), (0, 0), (0, 0), (half, Lq - L + half), (0, 0))
    ks = jnp.pad(strided(k), kpad)
    vs = jnp.pad(strided(v), kpad)
    key_idx = np.arange(nb)[:, None] * DSW_BLOCK + np.arange(K)[None, :]
    kb = ks[:, :, :, key_idx, :]
    vb = vs[:, :, :, key_idx, :]
    qb = qs.reshape(B, H, dilation, nb, DSW_BLOCK, Dh)
    off = np.arange(K)[None, :] - half - np.arange(DSW_BLOCK)[:, None]
    key_j = key_idx - half
    valid = ((np.abs(off)[None] <= half)
             & (key_j[:, None, :] >= 0) & (key_j[:, None, :] < L))
    s = jnp.einsum('bhrnqd,bhrnkd->bhrnqk', qb, kb) * (HEAD_DIM ** -0.5)
    dist = (np.abs(off) * dilation).astype(np.float32)
    s = s - (slopes[:, None, None] * dist[None])[None, :, None, None]
    s = jnp.where(valid[None, None, None], s, -jnp.inf)
    m = jnp.max(s, axis=-1, keepdims=True)
    p = jnp.exp(s - m)
    l = jnp.sum(p, axis=-1, keepdims=True)
    o = jnp.einsum('bhrnqk,bhrnkd->bhrnqd', p, vb) / l
    lse = (m + jnp.log(l))[..., 0]
    o = o.reshape(B, H, dilation, Lq, Dh)[:, :, :, :L].transpose(0, 1, 3, 2, 4).reshape(B, H, T, Dh)
    lse = lse.reshape(B, H, dilation, Lq)[..., :L].transpose(0, 1, 3, 2).reshape(B, H, T)
    return o, lse


def dilated_mixture_attention(q, k, v):
    H = q.shape[1]
    slopes = alibi_slopes(H)
    outs, lses = [], []
    for window, dilation in DILATED_PAIRS:
        o, lse = dilated_window_branch(q, k, v, slopes, window, dilation)
        outs.append(o)
        lses.append(lse)
    wts = jax.nn.softmax(jnp.stack(lses, axis=0), axis=0)
    y = wts[0][..., None] * outs[0]
    for i in range(1, len(outs)):
        y = y + wts[i][..., None] * outs[i]
    return y


def neighborhood_attention(q, k, v, rpb):
    B, H, T, Dh = q.shape
    rows = T // GRID_W
    kh = min(NA_ROWS, rows)
    n_cb = GRID_W // NA_COL_BLOCK
    qg = q.reshape(B, H, rows, GRID_W, Dh)
    kg = k.reshape(B, H, rows, GRID_W, Dh)
    vg = v.reshape(B, H, rows, GRID_W, Dh)
    cb_start = np.clip(np.arange(n_cb) * NA_COL_BLOCK - NA_COLS // 2, 0, GRID_W - NA_COL_REGION)
    key_cols = cb_start[:, None] + np.arange(NA_COL_REGION)[None, :]
    q_cols = np.arange(GRID_W).reshape(n_cb, NA_COL_BLOCK)
    wc = np.clip(q_cols - NA_COLS // 2, 0, GRID_W - NA_COLS)[..., None]
    kc = key_cols[:, None, :]
    col_valid = (kc >= wc) & (kc < wc + NA_COLS)
    col_bias_idx = np.clip(kc - q_cols[..., None] + NA_COLS - 1, 0, 2 * NA_COLS - 2)
    scale = HEAD_DIM ** -0.5

    def row_step(r):
        rs = jnp.clip(r - kh // 2, 0, rows - kh)
        k_rows = lax.dynamic_slice_in_dim(kg, rs, kh, axis=2)
        v_rows = lax.dynamic_slice_in_dim(vg, rs, kh, axis=2)
        q_row = lax.dynamic_index_in_dim(qg, r, axis=2, keepdims=False)
        k_blk = k_rows[:, :, :, key_cols, :]
        v_blk = v_rows[:, :, :, key_cols, :]
        q_blk = q_row.reshape(B, H, n_cb, NA_COL_BLOCK, Dh)
        s = jnp.einsum('bhcqd,bhrckd->bhcqrk', q_blk, k_blk) * scale
        row_idx = (rs + jnp.arange(kh) - r + NA_ROWS - 1)[None, None, :, None]
        bias = rpb[:, row_idx, col_bias_idx[:, :, None, :]]
        s = s + bias[None].astype(jnp.float32)
        s = jnp.where(col_valid[:, :, None, :], s, -jnp.inf)
        p = jax.nn.softmax(s.reshape(B, H, n_cb, NA_COL_BLOCK, kh * NA_COL_REGION), axis=-1)
        p = p.reshape(s.shape)
        o = jnp.einsum('bhcqrk,bhrckd->bhcqd', p, v_blk)
        return o.reshape(B, H, GRID_W, Dh)

    out = lax.map(row_step, jnp.arange(rows))
    return out.transpose(1, 2, 0, 3, 4).reshape(B, H, T, Dh)


def differential_attention(q, k, v, lam):
    B, H, _, T, Dh = q.shape
    slopes = alibi_slopes(H)
    nq = T // DIFF_BLOCK
    q_blocks = q.reshape(B, H, 2, nq, DIFF_BLOCK, Dh).transpose(3, 0, 1, 2, 4, 5)
    kpos = jnp.arange(T)
    scale = Dh ** -0.5

    def block(args):
        qb, start = args
        s = jnp.einsum('bhiqd,bhikd->bhiqk', qb, k) * scale
        qpos = start + jnp.arange(DIFF_BLOCK)
        dist = jnp.abs(qpos[:, None] - kpos[None, :]).astype(jnp.float32)
        s = s - (slopes[:, None, None] * dist[None])[None, :, None]
        p = jax.nn.softmax(s, axis=-1)
        a = p[:, :, 0] - lam * p[:, :, 1]
        return jnp.einsum('bhqk,bhkd->bhqd', a, v)

    out = lax.map(block, (q_blocks, jnp.arange(nq) * DIFF_BLOCK))
    return out.transpose(1, 2, 0, 3, 4).reshape(B, H, T, 2 * Dh)


def even_mixer(h, w_in, w_out, rpb):
    B, T, _ = h.shape
    proj = (h @ w_in).astype(jnp.float32)
    cuts = np.cumsum([W_A, W_A, W_A, W_B, W_B])
    qa, ka, va, qb, kb, vb = jnp.split(proj, cuts, axis=-1)

    def heads(a):
        return a.reshape(B, T, -1, HEAD_DIM).transpose(0, 2, 1, 3)

    ya = dilated_mixture_attention(heads(qa), heads(ka), heads(va))
    yb = neighborhood_attention(heads(qb), heads(kb), heads(vb), rpb)
    y = jnp.concatenate([ya, yb], axis=1).transpose(0, 2, 1, 3).reshape(B, T, W_A + W_B)
    return y.astype(h.dtype) @ w_out


def odd_mixer(h, w_qkv, w_out, lam_q1, lam_k1, lam_q2, lam_k2, subln, layer):
    B, T, _ = h.shape
    proj = (h @ w_qkv).astype(jnp.float32)
    q, k, v = jnp.split(proj, 3, axis=-1)
    q = q.reshape(B, T, N_HEADS_C, 2, HEAD_DIM).transpose(0, 2, 3, 1, 4)
    k = k.reshape(B, T, N_HEADS_C, 2, HEAD_DIM).transpose(0, 2, 3, 1, 4)
    v = v.reshape(B, T, N_HEADS_C, 2 * HEAD_DIM).transpose(0, 2, 1, 3)
    lam_init = lambda_init_fn(layer)
    f32 = jnp.float32
    lam = (jnp.exp(jnp.sum(lam_q1.astype(f32) * lam_k1.astype(f32)))
           - jnp.exp(jnp.sum(lam_q2.astype(f32) * lam_k2.astype(f32))) + lam_init)
    y = differential_attention(q, k, v, lam)
    y = rmsnorm(y, subln, SUBLN_EPS) * (1.0 - lam_init)
    y = y.transpose(0, 2, 1, 3).reshape(B, T, N_HEADS_C * 2 * HEAD_DIM)
    return y.astype(h.dtype) @ w_out


def setup_inputs(seed: int = 0) -> dict:
    key = jax.random.key(seed)
    ks = jax.random.split(key, 24)
    f32 = jnp.float32

    def w(k, shape, fan_in):
        return jax.random.normal(k, shape, f32) * fan_in ** -0.5

    def gain(k, n):
        return 1.0 + 0.02 * jax.random.normal(k, (n,), f32)

    D = D_MODEL
    WC = N_HEADS_C * 2 * HEAD_DIM
    return {
        "x": jax.random.normal(ks[0], (BATCH, SEQ, D), f32),
        "a0_norm": gain(ks[1], D),
        "a0_w_in": w(ks[2], (D, 3 * W_A + 3 * W_B), D),
        "a0_w_out": w(ks[3], (W_A + W_B, D), W_A + W_B),
        "a0_rpb": 0.1 * jax.random.normal(ks[4], (N_HEADS_B, 2 * NA_ROWS - 1, 2 * NA_COLS - 1), f32),
        "f0_norm": gain(ks[5], D),
        "f0_w_gate": w(ks[6], (D, D_FF), D),
        "f0_w_up": w(ks[7], (D, D_FF), D),
        "f0_w_down": w(ks[8], (D_FF, D), D_FF),
        "a1_norm": gain(ks[9], D),
        "a1_w_qkv": w(ks[10], (D, 3 * WC), D),
        "a1_w_out": w(ks[11], (WC, D), WC),
        "a1_lam_q1": 0.1 * jax.random.normal(ks[12], (HEAD_DIM,), f32),
        "a1_lam_k1": 0.1 * jax.random.normal(ks[13], (HEAD_DIM,), f32),
        "a1_lam_q2": 0.1 * jax.random.normal(ks[14], (HEAD_DIM,), f32),
        "a1_lam_k2": 0.1 * jax.random.normal(ks[15], (HEAD_DIM,), f32),
        "a1_subln": gain(ks[16], 2 * HEAD_DIM),
        "f1_norm": gain(ks[17], D),
        "f1_w_gate": w(ks[18], (D, D_FF), D),
        "f1_w_up": w(ks[19], (D, D_FF), D),
        "f1_w_down": w(ks[20], (D_FF, D), D_FF),
        "final_norm": gain(ks[21], D),
    }


def reference(x, a0_norm, a0_w_in, a0_w_out, a0_rpb, f0_norm, f0_w_gate, f0_w_up, f0_w_down,
              a1_norm, a1_w_qkv, a1_w_out, a1_lam_q1, a1_lam_k1, a1_lam_q2, a1_lam_k2, a1_subln,
              f1_norm, f1_w_gate, f1_w_up, f1_w_down, final_norm):
    ffn_params = [(f0_norm, f0_w_gate, f0_w_up, f0_w_down),
                  (f1_norm, f1_w_gate, f1_w_up, f1_w_down)]
    for layer in range(DEPTH):
        if layer % 2 == 0:
            x = x + even_mixer(rmsnorm(x, a0_norm), a0_w_in, a0_w_out, a0_rpb)
        else:
            x = x + odd_mixer(rmsnorm(x, a1_norm), a1_w_qkv, a1_w_out,
                              a1_lam_q1, a1_lam_k1, a1_lam_q2, a1_lam_k2, a1_subln, layer)
        g, wg, wu, wd = ffn_params[layer]
        x = x + swiglu(rmsnorm(x, g), wg, wu, wd)
    return rmsnorm(x, final_norm)
```

```python
import functools
import math

import numpy as np
import jax
import jax.numpy as jnp
from jax import lax
from jax.experimental import pallas as pl
from jax.experimental.pallas import tpu as pltpu

HEAD_DIM = 64
N_HEADS_A = 8
N_HEADS_B = 8
W_A = N_HEADS_A * HEAD_DIM
W_B = N_HEADS_B * HEAD_DIM
DILATED_PAIRS = ((128, 1), (512, 4), (2048, 16))
GRID_W = 64
NA_ROWS = 8
NA_COLS = 16
RMS_EPS = 1e-6
SUBLN_EPS = 1e-5
LOG2E = math.log2(math.e)

LANES = 128
V7X_VMEM_LIMIT_BYTES = 56 * 1024 * 1024

NEG_BIG = -1e30

BF16 = jnp.bfloat16
F32 = jnp.float32


def _params(semantics):
    return pltpu.CompilerParams(dimension_semantics=semantics,
                                vmem_limit_bytes=V7X_VMEM_LIMIT_BYTES)


def _rms(x, g, eps):
    ms = jnp.mean(x * x, axis=-1, keepdims=True)
    return (x * lax.rsqrt(ms + eps)) * g


def _dot_nt(a, b):
    return lax.dot_general(a, b, (((1,), (1,)), ((), ())), preferred_element_type=F32)


def _dot(a, b):
    return jnp.dot(a, b, preferred_element_type=F32)


def _norm_proj_kernel(x_ref, g_ref, w_ref, cs_ref, o_ref, *, n_chunk):
    h = _rms(x_ref[...], g_ref[...], RMS_EPS).astype(BF16)
    n = o_ref.shape[1]
    for c in range(0, n, n_chunk):
        acc = _dot(h, w_ref[:, c:c + n_chunk])
        o_ref[:, c:c + n_chunk] = (acc * cs_ref[:, c:c + n_chunk]).astype(o_ref.dtype)


def _norm_proj(x2, g, w, colscale, *, tm=512, n_chunk=1024):
    m, d = x2.shape
    n = w.shape[1]
    return pl.pallas_call(
        functools.partial(_norm_proj_kernel, n_chunk=n_chunk),
        grid=(m // tm,),
        in_specs=[
            pl.BlockSpec((tm, d), lambda i: (i, 0)),
            pl.BlockSpec((1, d), lambda i: (0, 0)),
            pl.BlockSpec((d, n), lambda i: (0, 0)),
            pl.BlockSpec((1, n), lambda i: (0, 0)),
        ],
        out_specs=pl.BlockSpec((tm, n), lambda i: (i, 0)),
        out_shape=jax.ShapeDtypeStruct((m, n), BF16),
        compiler_params=_params(("arbitrary",)),
        name="norm_proj",
    )(x2, g.reshape(1, d), w, colscale.reshape(1, n))


def _mix_ffn_kernel(*refs, n_parts, ff_chunk, final):
    x_ref = refs[0]
    part_refs = refs[1:1 + n_parts]
    wo_ref, g_ref, wg_ref, wu_ref, wd_ref = refs[1 + n_parts:6 + n_parts]
    fg_ref = refs[6 + n_parts] if final else None
    o_ref = refs[-1]

    y = jnp.concatenate([p_ref[...] for p_ref in part_refs], axis=1)
    x = x_ref[...] + _dot(y, wo_ref[...])
    h = _rms(x, g_ref[...], RMS_EPS).astype(BF16)
    d_ff = wg_ref.shape[1]
    acc = x
    for c in range(0, d_ff, ff_chunk):
        gate = _dot(h, wg_ref[:, c:c + ff_chunk])
        up = _dot(h, wu_ref[:, c:c + ff_chunk])
        a = (gate * jax.nn.sigmoid(gate) * up).astype(BF16)
        acc = acc + _dot(a, wd_ref[c:c + ff_chunk, :])
    if final:
        acc = _rms(acc, fg_ref[...], RMS_EPS)
    o_ref[...] = acc


def _mix_ffn(x2, parts, w_out, g, wg, wu, wd, final_g=None, *, tm=512, ff_chunk=256):
    m, d = x2.shape
    d_ff = wg.shape[1]
    final = final_g is not None
    const = lambda i: (0, 0)
    in_specs = [pl.BlockSpec((tm, d), lambda i: (i, 0))]
    in_specs += [pl.BlockSpec((tm, p.shape[1]), lambda i: (i, 0)) for p in parts]
    in_specs += [
        pl.BlockSpec((d, d), const),
        pl.BlockSpec((1, d), const),
        pl.BlockSpec((d, d_ff), const),
        pl.BlockSpec((d, d_ff), const),
        pl.BlockSpec((d_ff, d), const),
    ]
    args = [x2, *parts, w_out, g.reshape(1, d), wg, wu, wd]
    if final:
        in_specs.append(pl.BlockSpec((1, d), const))
        args.append(final_g.reshape(1, d))
    return pl.pallas_call(
        functools.partial(_mix_ffn_kernel, n_parts=len(parts), ff_chunk=ff_chunk, final=final),
        grid=(m // tm,),
        in_specs=in_specs,
        out_specs=pl.BlockSpec((tm, d), lambda i: (i, 0)),
        out_shape=jax.ShapeDtypeStruct((m, d), F32),
        compiler_params=_params(("arbitrary",)),
        name="mix_ffn",
    )(*args)


def _dilated_kernel(slope_ref, q_ref, k_ref, v_ref, *rest, dilation, half, q_blk, has_prev, last):
    if has_prev:
        po_ref, pl_ref = rest[0], rest[1]
        rest = rest[2:]
    if last:
        (y_ref,) = rest
    else:
        o_ref, l_ref = rest
    hp = pl.program_id(2)
    seq = q_ref.shape[1]
    k_win = min(q_blk + 2 * half, seq)
    lane = lax.broadcasted_iota(jnp.int32, (q_blk, LANES), 1)
    first_head = lane < HEAD_DIM
    rel = (lax.broadcasted_iota(jnp.int32, (q_blk, k_win), 1)
           - lax.broadcasted_iota(jnp.int32, (q_blk, k_win), 0))

    def block(qb, carry):
        q0 = pl.multiple_of(qb * q_blk, q_blk)
        ws = pl.multiple_of(jnp.clip(q0 - half, 0, seq - k_win), HEAD_DIM)
        q = q_ref[0, pl.ds(q0, q_blk), :]
        kw = k_ref[0, pl.ds(ws, k_win), :]
        vw = v_ref[0, pl.ds(ws, k_win), :]
        dist = jnp.abs(rel + (ws - q0))
        valid = dist <= half
        distf = dist.astype(F32) * float(dilation)
        outs, lses = [], []
        for head in range(2):
            slope2 = slope_ref[2 * hp + head] * LOG2E
            keep = first_head if head == 0 else jnp.logical_not(first_head)
            qh = jnp.where(keep, q, jnp.zeros_like(q))
            s = _dot_nt(qh, kw)
            s = jnp.where(valid, s - slope2 * distf, NEG_BIG)
            mx = jnp.max(s, axis=-1, keepdims=True)
            p = jnp.exp2(s - mx)
            den = jnp.sum(p, axis=-1, keepdims=True)
            outs.append(_dot(p.astype(BF16), vw) / den)
            lses.append(mx + jnp.log2(den))
        o = jnp.where(first_head, outs[0], outs[1])
        lse = jnp.where(first_head, lses[0], lses[1])
        if has_prev:
            o_prev = po_ref[0, pl.ds(q0, q_blk), :]
            l_prev = pl_ref[0, pl.ds(q0, q_blk), :]
            top = jnp.maximum(l_prev, lse)
            w_prev = jnp.exp2(l_prev - top)
            w_cur = jnp.exp2(lse - top)
            tot = w_prev + w_cur
            o = (o_prev * w_prev + o * w_cur) / tot
            lse = top + jnp.log2(tot)
        if last:
            y_ref[0, pl.ds(q0, q_blk), :] = o.astype(y_ref.dtype)
        else:
            o_ref[0, pl.ds(q0, q_blk), :] = o
            l_ref[0, pl.ds(q0, q_blk), :] = lse
        return carry

    lax.fori_loop(0, seq // q_blk, block, 0)


def _dilated_branch(slopes, qkv, prev, *, window, dilation, last, q_blk=128):
    b, t, n_cols = qkv.shape
    seq = t // dilation
    half = window // (2 * dilation)
    q_blk = min(q_blk, seq)
    assert seq % q_blk == 0 and q_blk % HEAD_DIM == 0 and half % HEAD_DIM == 0
    qkv_v = qkv.reshape(b, seq, dilation * n_cols)
    cb = n_cols // LANES
    ab = W_A // LANES
    seq_blk = (1, seq, LANES)
    in_specs = [
        pl.BlockSpec(memory_space=pltpu.SMEM),
        pl.BlockSpec(seq_blk, lambda bi, r, hp: (bi, 0, r * cb + hp)),
        pl.BlockSpec(seq_blk, lambda bi, r, hp: (bi, 0, r * cb + ab + hp)),
        pl.BlockSpec(seq_blk, lambda bi, r, hp: (bi, 0, r * cb + 2 * ab + hp)),
    ]
    args = [slopes, qkv_v, qkv_v, qkv_v]
    o_spec = pl.BlockSpec(seq_blk, lambda bi, r, hp: (bi, 0, r * ab + hp))
    has_prev = prev is not None
    if has_prev:
        in_specs += [o_spec, o_spec]
        args += [a.reshape(b, seq, dilation * W_A) for a in prev]
    if last:
        out_specs = o_spec
        out_shape = jax.ShapeDtypeStruct((b, seq, dilation * W_A), BF16)
    else:
        out_specs = [o_spec, o_spec]
        out_shape = [jax.ShapeDtypeStruct((b, seq, dilation * W_A), F32)] * 2
    res = pl.pallas_call(
        functools.partial(_dilated_kernel, dilation=dilation, half=half, q_blk=q_blk,
                          has_prev=has_prev, last=last),
        grid=(b, dilation, ab),
        in_specs=in_specs,
        out_specs=out_specs,
        out_shape=out_shape,
        compiler_params=_params(("arbitrary",) * 3),
        name=f"dilated_d{dilation}",
    )(*args)
    if last:
        return res.reshape(b, t, W_A)
    return tuple(a.reshape(b, t, W_A) for a in res)


def _natten_kernel(q_ref, k_ref, v_ref, tbl_ref, y_ref, *, rows):
    n_keys = NA_ROWS * GRID_W
    lane = lax.broadcasted_iota(jnp.int32, (GRID_W, LANES), 1)
    first_head = lane < HEAD_DIM

    def row_step(r, carry):
        rs = jnp.clip(r - NA_ROWS // 2, 0, rows - NA_ROWS)
        delta = r - rs
        q0 = pl.multiple_of(r * GRID_W, GRID_W)
        k0 = pl.multiple_of(rs * GRID_W, GRID_W)
        q = q_ref[0, pl.ds(q0, GRID_W), :]
        kw = k_ref[0, pl.ds(k0, n_keys), :]
        vw = v_ref[0, pl.ds(k0, n_keys), :]
        outs = []
        for head in range(2):
            keep = first_head if head == 0 else jnp.logical_not(first_head)
            qh = jnp.where(keep, q, jnp.zeros_like(q))
            s = _dot_nt(qh, kw) + tbl_ref[head, delta]
            mx = jnp.max(s, axis=-1, keepdims=True)
            p = jnp.exp2(s - mx)
            den = jnp.sum(p, axis=-1, keepdims=True)
            outs.append(_dot(p.astype(BF16), vw) / den)
        y_ref[0, pl.ds(q0, GRID_W), :] = jnp.where(first_head, outs[0], outs[1]).astype(y_ref.dtype)
        return carry

    lax.fori_loop(0, rows, row_step, 0)


def _natten_bias_table(rpb, rows):
    kh = NA_ROWS
    c = np.arange(GRID_W)[:, None]
    kc = np.arange(GRID_W)[None, :]
    wc = np.clip(c - NA_COLS // 2, 0, GRID_W - NA_COLS)
    col_valid = (kc >= wc) & (kc < wc + NA_COLS)
    col_idx = np.clip(kc - c + NA_COLS - 1, 0, 2 * NA_COLS - 2)
    delta = np.arange(kh)[:, None]
    kr = np.arange(kh)[None, :]
    row_idx = np.clip(kr - delta + NA_ROWS - 1, 0, 2 * NA_ROWS - 2)
    tbl = rpb[:, row_idx[:, :, None, None], col_idx[None, None]]
    tbl = jnp.where(col_valid[None, None, None], tbl * LOG2E, NEG_BIG)
    tbl = tbl.transpose(0, 1, 3, 2, 4)
    return tbl.reshape(rpb.shape[0], kh, GRID_W, kh * GRID_W).astype(F32)


def _natten(qkv, rpb):
    b, t, n_cols = qkv.shape
    rows = t // GRID_W
    assert rows >= NA_ROWS and t % GRID_W == 0
    tbl = _natten_bias_table(rpb, rows)
    ab = W_A // LANES
    bb = W_B // LANES
    base = 3 * ab
    seq_blk = (1, t, LANES)
    return pl.pallas_call(
        functools.partial(_natten_kernel, rows=rows),
        grid=(b, bb),
        in_specs=[
            pl.BlockSpec(seq_blk, lambda bi, hp: (bi, 0, base + hp)),
            pl.BlockSpec(seq_blk, lambda bi, hp: (bi, 0, base + bb + hp)),
            pl.BlockSpec(seq_blk, lambda bi, hp: (bi, 0, base + 2 * bb + hp)),
            pl.BlockSpec((2, NA_ROWS, GRID_W, NA_ROWS * GRID_W), lambda bi, hp: (hp, 0, 0, 0)),
        ],
        out_specs=pl.BlockSpec(seq_blk, lambda bi, hp: (bi, 0, hp)),
        out_shape=jax.ShapeDtypeStruct((b, t, W_B), BF16),
        compiler_params=_params(("arbitrary",) * 2),
        name="natten",
    )(qkv, qkv, qkv, tbl)


def _diff_attn_kernel(slope_ref, q_ref, k_ref, v_ref, lq1_ref, lk1_ref, lq2_ref, lk2_ref,
                      subln_ref, y_ref, qq_ref, rel_ref, m_ref, l_ref, acc_ref,
                      *, tq, tk, lam_init):
    h = pl.program_id(1)
    qi = pl.program_id(2)
    seq = k_ref.shape[1]
    n_kv = seq // tk
    slope2 = slope_ref[h] * LOG2E
    i0 = qi * tq

    q = q_ref[0]
    lane = lax.broadcasted_iota(jnp.int32, (tq, LANES), 1)
    zero = jnp.zeros_like(q)
    qq_ref[0:tq, :] = jnp.where(lane < HEAD_DIM, q, zero)
    qq_ref[tq:2 * tq, :] = jnp.where(lane < HEAD_DIM, zero, q)
    row = lax.broadcasted_iota(jnp.int32, (2 * tq, tk), 0)
    col = lax.broadcasted_iota(jnp.int32, (2 * tq, tk), 1)
    row = jnp.where(row >= tq, row - tq, row)
    rel_ref[...] = (col - row).astype(F32) * slope2
    m_ref[...] = jnp.full(m_ref.shape, NEG_BIG, F32)
    l_ref[...] = jnp.zeros(l_ref.shape, F32)
    acc_ref[...] = jnp.zeros(acc_ref.shape, F32)

    def step(kj, mode):
        j0 = pl.multiple_of(kj * tk, tk)
        kb = k_ref[0, pl.ds(j0, tk), :]
        vb = v_ref[0, pl.ds(j0, tk), :]
        s = _dot_nt(qq_ref[...], kb)
        off = (j0 - i0).astype(F32) * slope2
        if mode == "before":
            s = s + rel_ref[...]
            shift = off
        elif mode == "after":
            s = s - rel_ref[...]
            shift = -off
        else:
            s = s - jnp.abs(rel_ref[...] + off)
            shift = jnp.zeros((), F32)
        m_prev = m_ref[...]
        m_new = jnp.maximum(m_prev, jnp.max(s, axis=-1, keepdims=True) + shift)
        alpha = jnp.exp2(m_prev - m_new)
        p = jnp.exp2(s - (m_new - shift))
        l_ref[...] = alpha * l_ref[...] + jnp.sum(p, axis=-1, keepdims=True)
        acc_ref[...] = alpha * acc_ref[...] + _dot(p.astype(BF16), vb)
        m_ref[...] = m_new

    lo = i0 // tk
    hi = (i0 + tq + tk - 1) // tk

    def run(mode):
        def body(kj, carry):
            step(kj, mode)
            return carry
        return body

    lax.fori_loop(0, lo, run("before"), 0)
    lax.fori_loop(lo, hi, run("mixed"), 0)
    lax.fori_loop(hi, n_kv, run("after"), 0)

    lam = (jnp.exp(jnp.sum(lq1_ref[...] * lk1_ref[...], axis=-1, keepdims=True))
           - jnp.exp(jnp.sum(lq2_ref[...] * lk2_ref[...], axis=-1, keepdims=True)) + lam_init)
    o = acc_ref[...] / l_ref[...]
    y = o[0:tq, :] - lam * o[tq:2 * tq, :]
    y = _rms(y, subln_ref[...], SUBLN_EPS) * (1.0 - lam_init)
    y_ref[0] = y.astype(y_ref.dtype)


def _diff_attn(slopes, qkv, lq1, lk1, lq2, lk2, subln, lam_init, *, tq=256, tk=512):
    b, t, n_cols = qkv.shape
    nh = n_cols // (3 * LANES)
    assert t % tq == 0 and t % tk == 0
    vec = lambda a: a.reshape(1, -1).astype(F32)
    small = lambda n: pl.BlockSpec((1, n), lambda bi, h, qi: (0, 0))
    return pl.pallas_call(
        functools.partial(_diff_attn_kernel, tq=tq, tk=tk, lam_init=lam_init),
        grid=(b, nh, t // tq),
        in_specs=[
            pl.BlockSpec(memory_space=pltpu.SMEM),
            pl.BlockSpec((1, tq, LANES), lambda bi, h, qi: (bi, qi, h)),
            pl.BlockSpec((1, t, LANES), lambda bi, h, qi: (bi, 0, nh + h)),
            pl.BlockSpec((1, t, LANES), lambda bi, h, qi: (bi, 0, 2 * nh + h)),
            small(HEAD_DIM), small(HEAD_DIM), small(HEAD_DIM), small(HEAD_DIM),
            small(2 * HEAD_DIM),
        ],
        out_specs=pl.BlockSpec((1, tq, LANES), lambda bi, h, qi: (bi, qi, h)),
        out_shape=jax.ShapeDtypeStruct((b, t, nh * LANES), BF16),
        scratch_shapes=[
            pltpu.VMEM((2 * tq, LANES), BF16),
            pltpu.VMEM((2 * tq, tk), F32),
            pltpu.VMEM((2 * tq, 1), F32),
            pltpu.VMEM((2 * tq, 1), F32),
            pltpu.VMEM((2 * tq, LANES), F32),
        ],
        compiler_params=_params(("arbitrary",) * 3),
        name="diff_attn",
    )(slopes, qkv, qkv, qkv, vec(lq1), vec(lk1), vec(lq2), vec(lk2), vec(subln))


def _alibi_slopes(n):
    return jnp.exp2(-8.0 * jnp.arange(1, n + 1, dtype=F32) / n)


def _lambda_init(layer):
    return 0.8 - 0.6 * math.exp(-0.3 * layer)


def kernel(x, a0_norm, a0_w_in, a0_w_out, a0_rpb, f0_norm, f0_w_gate, f0_w_up, f0_w_down,
           a1_norm, a1_w_qkv, a1_w_out, a1_lam_q1, a1_lam_k1, a1_lam_q2, a1_lam_k2, a1_subln,
           f1_norm, f1_w_gate, f1_w_up, f1_w_down, final_norm):
    b, t, d = x.shape
    x2 = x.reshape(b * t, d)
    qscale = HEAD_DIM ** -0.5 * LOG2E
    bf = lambda w: w.astype(BF16)

    cs0 = np.ones((3 * W_A + 3 * W_B,), np.float32)
    cs0[0:W_A] = qscale
    cs0[3 * W_A:3 * W_A + W_B] = qscale
    qkv0 = _norm_proj(x2, a0_norm, bf(a0_w_in), jnp.asarray(cs0)).reshape(b, t, -1)
    slopes_a = _alibi_slopes(N_HEADS_A)
    state = None
    for idx, (window, dilation) in enumerate(DILATED_PAIRS):
        state = _dilated_branch(slopes_a, qkv0, state, window=window, dilation=dilation,
                                last=idx == len(DILATED_PAIRS) - 1)
    ya = state.reshape(b * t, W_A)
    yb = _natten(qkv0, a0_rpb).reshape(b * t, W_B)
    x2 = _mix_ffn(x2, [ya, yb], bf(a0_w_out), f0_norm, bf(f0_w_gate), bf(f0_w_up), bf(f0_w_down))

    wc = a1_w_qkv.shape[1] // 3
    cs1 = np.ones((3 * wc,), np.float32)
    cs1[0:wc] = qscale
    qkv1 = _norm_proj(x2, a1_norm, bf(a1_w_qkv), jnp.asarray(cs1)).reshape(b, t, -1)
    n_heads_c = wc // (2 * HEAD_DIM)
    y = _diff_attn(_alibi_slopes(n_heads_c), qkv1, a1_lam_q1, a1_lam_k1, a1_lam_q2, a1_lam_k2,
                   a1_subln, _lambda_init(1)).reshape(b * t, wc)
    out = _mix_ffn(x2, [y], bf(a1_w_out), f1_norm, bf(f1_w_gate), bf(f1_w_up), bf(f1_w_down),
                   final_norm)
    return out.reshape(b, t, d)
```

```python
import functools
import math

import numpy as np
import jax
import jax.numpy as jnp
from jax import lax
from jax.experimental import pallas as pl
from jax.experimental.pallas import tpu as pltpu

HEAD_DIM = 64
N_HEADS_A = 8
N_HEADS_B = 8
W_A = N_HEADS_A * HEAD_DIM
W_B = N_HEADS_B * HEAD_DIM
DILATED_PAIRS = ((128, 1), (512, 4), (2048, 16))
GRID_W = 64
NA_ROWS = 8
NA_COLS = 16
RMS_EPS = 1e-6
SUBLN_EPS = 1e-5
LOG2E = math.log2(math.e)

LANES = 128
V7X_VMEM_LIMIT_BYTES = 56 * 1024 * 1024

NEG_BIG = -1e30

BF16 = jnp.bfloat16
F32 = jnp.float32


def _params(semantics):
    return pltpu.CompilerParams(dimension_semantics=semantics,
                                vmem_limit_bytes=V7X_VMEM_LIMIT_BYTES)


def _rms(x, g, eps):
    ms = jnp.mean(x * x, axis=-1, keepdims=True)
    return (x * lax.rsqrt(ms + eps)) * g


def _dot_nt(a, b):
    return lax.dot_general(a, b, (((1,), (1,)), ((), ())), preferred_element_type=F32)


def _dot(a, b):
    return jnp.dot(a, b, preferred_element_type=F32)


def _norm_proj_kernel(x_ref, g_ref, w_ref, cs_ref, o_ref, *, n_chunk):
    h = _rms(x_ref[...], g_ref[...], RMS_EPS).astype(BF16)
    n = o_ref.shape[1]
    for c in range(0, n, n_chunk):
        acc = _dot(h, w_ref[:, c:c + n_chunk])
        o_ref[:, c:c + n_chunk] = (acc * cs_ref[:, c:c + n_chunk]).astype(o_ref.dtype)


def _norm_proj(x2, g, w, colscale, *, tm=512, n_chunk=1024):
    m, d = x2.shape
    n = w.shape[1]
    return pl.pallas_call(
        functools.partial(_norm_proj_kernel, n_chunk=n_chunk),
        grid=(m // tm,),
        in_specs=[
            pl.BlockSpec((tm, d), lambda i: (i, 0)),
            pl.BlockSpec((1, d), lambda i: (0, 0)),
            pl.BlockSpec((d, n), lambda i: (0, 0)),
            pl.BlockSpec((1, n), lambda i: (0, 0)),
        ],
        out_specs=pl.BlockSpec((tm, n), lambda i: (i, 0)),
        out_shape=jax.ShapeDtypeStruct((m, n), BF16),
        compiler_params=_params(("arbitrary",)),
        name="norm_proj",
    )(x2, g.reshape(1, d), w, colscale.reshape(1, n))


def _mix_ffn_kernel(*refs, n_parts, ff_chunk, final):
    x_ref = refs[0]
    part_refs = refs[1:1 + n_parts]
    wo_ref, g_ref, wg_ref, wu_ref, wd_ref = refs[1 + n_parts:6 + n_parts]
    fg_ref = refs[6 + n_parts] if final else None
    o_ref = refs[-1]

    y = jnp.concatenate([p_ref[...] for p_ref in part_refs], axis=1)
    x = x_ref[...] + _dot(y, wo_ref[...])
    h = _rms(x, g_ref[...], RMS_EPS).astype(BF16)
    d_ff = wg_ref.shape[1]
    acc = x
    for c in range(0, d_ff, ff_chunk):
        gate = _dot(h, wg_ref[:, c:c + ff_chunk])
        up = _dot(h, wu_ref[:, c:c + ff_chunk])
        a = (gate * jax.nn.sigmoid(gate) * up).astype(BF16)
        acc = acc + _dot(a, wd_ref[c:c + ff_chunk, :])
    if final:
        acc = _rms(acc, fg_ref[...], RMS_EPS)
    o_ref[...] = acc


def _mix_ffn(x2, parts, w_out, g, wg, wu, wd, final_g=None, *, tm=512, ff_chunk=256):
    m, d = x2.shape
    d_ff = wg.shape[1]
    final = final_g is not None
    const = lambda i: (0, 0)
    in_specs = [pl.BlockSpec((tm, d), lambda i: (i, 0))]
    in_specs += [pl.BlockSpec((tm, p.shape[1]), lambda i: (i, 0)) for p in parts]
    in_specs += [
        pl.BlockSpec((d, d), const),
        pl.BlockSpec((1, d), const),
        pl.BlockSpec((d, d_ff), const),
        pl.BlockSpec((d, d_ff), const),
        pl.BlockSpec((d_ff, d), const),
    ]
    args = [x2, *parts, w_out, g.reshape(1, d), wg, wu, wd]
    if final:
        in_specs.append(pl.BlockSpec((1, d), const))
        args.append(final_g.reshape(1, d))
    return pl.pallas_call(
        functools.partial(_mix_ffn_kernel, n_parts=len(parts), ff_chunk=ff_chunk, final=final),
        grid=(m // tm,),
        in_specs=in_specs,
        out_specs=pl.BlockSpec((tm, d), lambda i: (i, 0)),
        out_shape=jax.ShapeDtypeStruct((m, d), F32),
        compiler_params=_params(("arbitrary",)),
        name="mix_ffn",
    )(*args)


def _dilated_kernel(slope_ref, q_ref, k_ref, v_ref, *rest, dilation, half, q_blk, has_prev, last):
    if has_prev:
        po_ref, pl_ref = rest[0], rest[1]
        rest = rest[2:]
    if last:
        (y_ref,) = rest
    else:
        o_ref, l_ref = rest
    hp = pl.program_id(2)
    seq = q_ref.shape[1]
    k_win = min(q_blk + 2 * half, seq)
    lane = lax.broadcasted_iota(jnp.int32, (q_blk, LANES), 1)
    first_head = lane < HEAD_DIM
    rel = (lax.broadcasted_iota(jnp.int32, (q_blk, k_win), 1)
           - lax.broadcasted_iota(jnp.int32, (q_blk, k_win), 0))

    def block(qb, carry):
        q0 = pl.multiple_of(qb * q_blk, q_blk)
        ws = pl.multiple_of(jnp.clip(q0 - half, 0, seq - k_win), HEAD_DIM)
        q = q_ref[0, pl.ds(q0, q_blk), :]
        kw = k_ref[0, pl.ds(ws, k_win), :]
        vw = v_ref[0, pl.ds(ws, k_win), :]
        dist = jnp.abs(rel + (ws - q0))
        valid = dist <= half
        distf = dist.astype(F32) * float(dilation)
        outs, lses = [], []
        for head in range(2):
            slope2 = slope_ref[2 * hp + head] * LOG2E
            keep = first_head if head == 0 else jnp.logical_not(first_head)
            qh = jnp.where(keep, q, jnp.zeros_like(q))
            s = _dot_nt(qh, kw)
            s = jnp.where(valid, s - slope2 * distf, NEG_BIG)
            mx = jnp.max(s, axis=-1, keepdims=True)
            p = jnp.exp2(s - mx)
            den = jnp.sum(p, axis=-1, keepdims=True)
            outs.append(_dot(p.astype(BF16), vw) / den)
            lses.append(mx + jnp.log2(den))
        o = jnp.where(first_head, outs[0], outs[1])
        lse = jnp.where(first_head, lses[0], lses[1])
        if has_prev:
            o_prev = po_ref[0, pl.ds(q0, q_blk), :]
            l_prev = pl_ref[0, pl.ds(q0, q_blk), :]
            top = jnp.maximum(l_prev, lse)
            w_prev = jnp.exp2(l_prev - top)
            w_cur = jnp.exp2(lse - top)
            tot = w_prev + w_cur
            o = (o_prev * w_prev + o * w_cur) / tot
            lse = top + jnp.log2(tot)
        if last:
            y_ref[0, pl.ds(q0, q_blk), :] = o.astype(y_ref.dtype)
        else:
            o_ref[0, pl.ds(q0, q_blk), :] = o
            l_ref[0, pl.ds(q0, q_blk), :] = lse
        return carry

    lax.fori_loop(0, seq // q_blk, block, 0)


def _dilated_branch(slopes, qkv, prev, *, window, dilation, last, q_blk=128):
    b, t, n_cols = qkv.shape
    seq = t // dilation
    half = window // (2 * dilation)
    q_blk = min(q_blk, seq)
    assert seq % q_blk == 0 and q_blk % HEAD_DIM == 0 and half % HEAD_DIM == 0
    qkv_v = qkv.reshape(b, seq, dilation * n_cols)
    cb = n_cols // LANES
    ab = W_A // LANES
    seq_blk = (1, seq, LANES)
    in_specs = [
        pl.BlockSpec(memory_space=pltpu.SMEM),
        pl.BlockSpec(seq_blk, lambda bi, r, hp: (bi, 0, r * cb + hp)),
        pl.BlockSpec(seq_blk, lambda bi, r, hp: (bi, 0, r * cb + ab + hp)),
        pl.BlockSpec(seq_blk, lambda bi, r, hp: (bi, 0, r * cb + 2 * ab + hp)),
    ]
    args = [slopes, qkv_v, qkv_v, qkv_v]
    o_spec = pl.BlockSpec(seq_blk, lambda bi, r, hp: (bi, 0, r * ab + hp))
    has_prev = prev is not None
    if has_prev:
        in_specs += [o_spec, o_spec]
        args += [a.reshape(b, seq, dilation * W_A) for a in prev]
    if last:
        out_specs = o_spec
        out_shape = jax.ShapeDtypeStruct((b, seq, dilation * W_A), BF16)
    else:
        out_specs = [o_spec, o_spec]
        out_shape = [jax.ShapeDtypeStruct((b, seq, dilation * W_A), F32)] * 2
    res = pl.pallas_call(
        functools.partial(_dilated_kernel, dilation=dilation, half=half, q_blk=q_blk,
                          has_prev=has_prev, last=last),
        grid=(b, dilation, ab),
        in_specs=in_specs,
        out_specs=out_specs,
        out_shape=out_shape,
        compiler_params=_params(("arbitrary",) * 3),
        name=f"dilated_d{dilation}",
    )(*args)
    if last:
        return res.reshape(b, t, W_A)
    return tuple(a.reshape(b, t, W_A) for a in res)


def _natten_kernel(q_ref, k_ref, v_ref, tbl_ref, y_ref, *, rows):
    n_keys = NA_ROWS * GRID_W
    lane = lax.broadcasted_iota(jnp.int32, (GRID_W, LANES), 1)
    first_head = lane < HEAD_DIM

    def row_step(r, carry):
        rs = jnp.clip(r - NA_ROWS // 2, 0, rows - NA_ROWS)
        delta = r - rs
        q0 = pl.multiple_of(r * GRID_W, GRID_W)
        k0 = pl.multiple_of(rs * GRID_W, GRID_W)
        q = q_ref[0, pl.ds(q0, GRID_W), :]
        kw = k_ref[0, pl.ds(k0, n_keys), :]
        vw = v_ref[0, pl.ds(k0, n_keys), :]
        outs = []
        for head in range(2):
            keep = first_head if head == 0 else jnp.logical_not(first_head)
            qh = jnp.where(keep, q, jnp.zeros_like(q))
            s = _dot_nt(qh, kw) + tbl_ref[head, delta]
            mx = jnp.max(s, axis=-1, keepdims=True)
            p = jnp.exp2(s - mx)
            den = jnp.sum(p, axis=-1, keepdims=True)
            outs.append(_dot(p.astype(BF16), vw) / den)
        y_ref[0, pl.ds(q0, GRID_W), :] = jnp.where(first_head, outs[0], outs[1]).astype(y_ref.dtype)
        return carry

    lax.fori_loop(0, rows, row_step, 0)


def _natten_bias_table(rpb, rows):
    kh = NA_ROWS
    c = np.arange(GRID_W)[:, None]
    kc = np.arange(GRID_W)[None, :]
    wc = np.clip(c - NA_COLS // 2, 0, GRID_W - NA_COLS)
    col_valid = (kc >= wc) & (kc < wc + NA_COLS)
    col_idx = np.clip(kc - c + NA_COLS - 1, 0, 2 * NA_COLS - 2)
    delta = np.arange(kh)[:, None]
    kr = np.arange(kh)[None, :]
    row_idx = np.clip(kr - delta + NA_ROWS - 1, 0, 2 * NA_ROWS - 2)
    tbl = rpb[:, row_idx[:, :, None, None], col_idx[None, None]]
    tbl = jnp.where(col_valid[None, None, None], tbl * LOG2E, NEG_BIG)
    tbl = tbl.transpose(0, 1, 3, 2, 4)
    return tbl.reshape(rpb.shape[0], kh, GRID_W, kh * GRID_W).astype(F32)


def _natten(qkv, rpb):
    b, t, n_cols = qkv.shape
    rows = t // GRID_W
    assert rows >= NA_ROWS and t % GRID_W == 0
    tbl = _natten_bias_table(rpb, rows)
    ab = W_A // LANES
    bb = W_B // LANES
    base = 3 * ab
    seq_blk = (1, t, LANES)
    return pl.pallas_call(
        functools.partial(_natten_kernel, rows=rows),
        grid=(b, bb),
        in_specs=[
            pl.BlockSpec(seq_blk, lambda bi, hp: (bi, 0, base + hp)),
            pl.BlockSpec(seq_blk, lambda bi, hp: (bi, 0, base + bb + hp)),
            pl.BlockSpec(seq_blk, lambda bi, hp: (bi, 0, base + 2 * bb + hp)),
            pl.BlockSpec((2, NA_ROWS, GRID_W, NA_ROWS * GRID_W), lambda bi, hp: (hp, 0, 0, 0)),
        ],
        out_specs=pl.BlockSpec(seq_blk, lambda bi, hp: (bi, 0, hp)),
        out_shape=jax.ShapeDtypeStruct((b, t, W_B), BF16),
        compiler_params=_params(("arbitrary",) * 2),
        name="natten",
    )(qkv, qkv, qkv, tbl)


def _diff_attn_kernel(slope_ref, q_ref, k_ref, v_ref, lq1_ref, lk1_ref, lq2_ref, lk2_ref,
                      subln_ref, y_ref, qq_ref, rel_ref, vt_ref, s_ref, mx_ref,
                      m_ref, l_ref, acc_ref, *, tq, tk, lam_init):
    h = pl.program_id(1)
    qi = pl.program_id(2)
    n_kv = k_ref.shape[1] // tk
    slope2 = slope_ref[h] * LOG2E
    i0 = qi * tq
    diag = i0 // tk

    @pl.when(qi == 0)
    def _per_head_setup():
        key = lax.broadcasted_iota(jnp.int32, (tk, 2 * tq), 0)
        qry = lax.broadcasted_iota(jnp.int32, (tk, 2 * tq), 1)
        qry = jnp.where(qry >= tq, qry - tq, qry)
        rel = (key - qry).astype(F32) * slope2
        rel_ref[0] = rel
        rel_ref[1] = -rel
        for c in range(n_kv):
            vt_ref[c] = v_ref[0, c * tk:(c + 1) * tk, :].astype(F32).T.astype(BF16)

    q = q_ref[0]
    lane = lax.broadcasted_iota(jnp.int32, (tq, LANES), 1)
    zero = jnp.zeros_like(q)
    qq_ref[0:tq, :] = jnp.where(lane < HEAD_DIM, q, zero)
    qq_ref[tq:2 * tq, :] = jnp.where(lane < HEAD_DIM, zero, q)

    m_ref[...] = jnp.full(m_ref.shape, NEG_BIG, F32)
    l_ref[...] = jnp.zeros(l_ref.shape, F32)
    acc_ref[...] = jnp.zeros(acc_ref.shape, F32)

    def block_offset(j):
        return (j * tk - i0).astype(F32) * slope2

    def produce(j, slot, overlapping):
        j0 = pl.multiple_of(j * tk, tk)
        s = _dot_nt(k_ref[0, pl.ds(j0, tk), :], qq_ref[...])
        if overlapping:
            s = s - jnp.abs(rel_ref[0] + block_offset(j))
        else:
            s = s + rel_ref[(j > diag).astype(jnp.int32)]
        s_ref[slot] = s
        mx_ref[slot] = jnp.max(s, axis=0, keepdims=True)

    def consume(j, slot, overlapping):
        if overlapping:
            shift = jnp.zeros((), F32)
        else:
            off = block_offset(j)
            shift = jnp.where(j > diag, -off, off)
        m_prev = m_ref[...]
        m_new = jnp.maximum(m_prev, mx_ref[slot] + shift)
        alpha = jnp.exp2(m_prev - m_new)
        p = jnp.exp2(s_ref[slot] - (m_new - shift))
        l_ref[...] = alpha * l_ref[...] + jnp.sum(p, axis=0, keepdims=True)
        acc_ref[...] = alpha * acc_ref[...] + _dot(vt_ref[j], p.astype(BF16))
        m_ref[...] = m_new

    def block_at(t):
        if t == 0:
            return diag
        return (t - 1) + ((t - 1) >= diag).astype(jnp.int32)

    produce(block_at(0), 0, True)
    for t in range(n_kv):
        if t + 1 < n_kv:
            produce(block_at(t + 1), (t + 1) % 2, False)
        consume(block_at(t), t % 2, t == 0)

    lam = (jnp.exp(jnp.sum(lq1_ref[...] * lk1_ref[...], axis=-1, keepdims=True))
           - jnp.exp(jnp.sum(lq2_ref[...] * lk2_ref[...], axis=-1, keepdims=True)) + lam_init)
    o = acc_ref[...] / l_ref[...]
    y = (o[:, 0:tq] - lam * o[:, tq:2 * tq]).T
    y = _rms(y, subln_ref[...], SUBLN_EPS) * (1.0 - lam_init)
    y_ref[0] = y.astype(y_ref.dtype)


def _diff_attn(slopes, qkv, lq1, lk1, lq2, lk2, subln, lam_init, *, tq=256, tk=512):
    b, t, n_cols = qkv.shape
    nh = n_cols // (3 * LANES)
    assert t % tk == 0 and tk % tq == 0
    vec = lambda a: a.reshape(1, -1).astype(F32)
    small = lambda n: pl.BlockSpec((1, n), lambda bi, h, qi: (0, 0))
    return pl.pallas_call(
        functools.partial(_diff_attn_kernel, tq=tq, tk=tk, lam_init=lam_init),
        grid=(b, nh, t // tq),
        in_specs=[
            pl.BlockSpec(memory_space=pltpu.SMEM),
            pl.BlockSpec((1, tq, LANES), lambda bi, h, qi: (bi, qi, h)),
            pl.BlockSpec((1, t, LANES), lambda bi, h, qi: (bi, 0, nh + h)),
            pl.BlockSpec((1, t, LANES), lambda bi, h, qi: (bi, 0, 2 * nh + h)),
            small(HEAD_DIM), small(HEAD_DIM), small(HEAD_DIM), small(HEAD_DIM),
            small(2 * HEAD_DIM),
        ],
        out_specs=pl.BlockSpec((1, tq, LANES), lambda bi, h, qi: (bi, qi, h)),
        out_shape=jax.ShapeDtypeStruct((b, t, nh * LANES), BF16),
        scratch_shapes=[
            pltpu.VMEM((2 * tq, LANES), BF16),
            pltpu.VMEM((2, tk, 2 * tq), F32),
            pltpu.VMEM((t // tk, LANES, tk), BF16),
            pltpu.VMEM((2, tk, 2 * tq), F32),
            pltpu.VMEM((2, 1, 2 * tq), F32),
            pltpu.VMEM((1, 2 * tq), F32),
            pltpu.VMEM((1, 2 * tq), F32),
            pltpu.VMEM((LANES, 2 * tq), F32),
        ],
        compiler_params=_params(("arbitrary",) * 3),
        name="diff_attn",
    )(slopes, qkv, qkv, qkv, vec(lq1), vec(lk1), vec(lq2), vec(lk2), vec(subln))


def _alibi_slopes(n):
    return jnp.exp2(-8.0 * jnp.arange(1, n + 1, dtype=F32) / n)


def _lambda_init(layer):
    return 0.8 - 0.6 * math.exp(-0.3 * layer)


def kernel(x, a0_norm, a0_w_in, a0_w_out, a0_rpb, f0_norm, f0_w_gate, f0_w_up, f0_w_down,
           a1_norm, a1_w_qkv, a1_w_out, a1_lam_q1, a1_lam_k1, a1_lam_q2, a1_lam_k2, a1_subln,
           f1_norm, f1_w_gate, f1_w_up, f1_w_down, final_norm):
    b, t, d = x.shape
    x2 = x.reshape(b * t, d)
    qscale = HEAD_DIM ** -0.5 * LOG2E
    bf = lambda w: w.astype(BF16)

    cs0 = np.ones((3 * W_A + 3 * W_B,), np.float32)
    cs0[0:W_A] = qscale
    cs0[3 * W_A:3 * W_A + W_B] = qscale
    qkv0 = _norm_proj(x2, a0_norm, bf(a0_w_in), jnp.asarray(cs0)).reshape(b, t, -1)
    slopes_a = _alibi_slopes(N_HEADS_A)
    state = None
    for idx, (window, dilation) in enumerate(DILATED_PAIRS):
        state = _dilated_branch(slopes_a, qkv0, state, window=window, dilation=dilation,
                                last=idx == len(DILATED_PAIRS) - 1)
    ya = state.reshape(b * t, W_A)
    yb = _natten(qkv0, a0_rpb).reshape(b * t, W_B)
    x2 = _mix_ffn(x2, [ya, yb], bf(a0_w_out), f0_norm, bf(f0_w_gate), bf(f0_w_up), bf(f0_w_down))

    wc = a1_w_qkv.shape[1] // 3
    cs1 = np.ones((3 * wc,), np.float32)
    cs1[0:wc] = qscale
    qkv1 = _norm_proj(x2, a1_norm, bf(a1_w_qkv), jnp.asarray(cs1)).reshape(b, t, -1)
    n_heads_c = wc // (2 * HEAD_DIM)
    y = _diff_attn(_alibi_slopes(n_heads_c), qkv1, a1_lam_q1, a1_lam_k1, a1_lam_q2, a1_lam_k2,
                   a1_subln, _lambda_init(1)).reshape(b * t, wc)
    out = _mix_ffn(x2, [y], bf(a1_w_out), f1_norm, bf(f1_w_gate), bf(f1_w_up), bf(f1_w_down),
                   final_norm)
    return out.reshape(b, t, d)
```

```python
import functools
import math

import numpy as np
import jax
import jax.numpy as jnp
from jax import lax
from jax.experimental import pallas as pl
from jax.experimental.pallas import tpu as pltpu

HEAD_DIM = 64
N_HEADS_A = 8
N_HEADS_B = 8
W_A = N_HEADS_A * HEAD_DIM
W_B = N_HEADS_B * HEAD_DIM
DILATED_PAIRS = ((128, 1), (512, 4), (2048, 16))
GRID_W = 64
NA_ROWS = 8
NA_COLS = 16
RMS_EPS = 1e-6
SUBLN_EPS = 1e-5
LOG2E = math.log2(math.e)

LANES = 128
V7X_VMEM_LIMIT_BYTES = 56 * 1024 * 1024

NEG_BIG = -1e30

BF16 = jnp.bfloat16
F32 = jnp.float32


def _params(semantics):
    return pltpu.CompilerParams(dimension_semantics=semantics,
                                vmem_limit_bytes=V7X_VMEM_LIMIT_BYTES)


def _rms(x, g, eps):
    ms = jnp.mean(x * x, axis=-1, keepdims=True)
    return (x * lax.rsqrt(ms + eps)) * g


def _dot_nt(a, b):
    return lax.dot_general(a, b, (((1,), (1,)), ((), ())), preferred_element_type=F32)


def _dot(a, b):
    return jnp.dot(a, b, preferred_element_type=F32)


def _norm_proj_kernel(x_ref, g_ref, w_ref, cs_ref, *o_refs, n_chunk):
    h = _rms(x_ref[...], g_ref[...], RMS_EPS).astype(BF16)
    col = 0
    for o_ref in o_refs:
        for c in range(0, o_ref.shape[1], n_chunk):
            acc = _dot(h, w_ref[:, col + c:col + c + n_chunk])
            o_ref[:, c:c + n_chunk] = (acc * cs_ref[:, col + c:col + c + n_chunk]).astype(o_ref.dtype)
        col += o_ref.shape[1]


def _norm_proj(x2, g, w, colscale, outs, *, tm=512, n_chunk=512):
    m, d = x2.shape
    n = w.shape[1]
    assert sum(c for c, _ in outs) == n and all(c % n_chunk == 0 for c, _ in outs)
    return pl.pallas_call(
        functools.partial(_norm_proj_kernel, n_chunk=n_chunk),
        grid=(m // tm,),
        in_specs=[
            pl.BlockSpec((tm, d), lambda i: (i, 0)),
            pl.BlockSpec((1, d), lambda i: (0, 0)),
            pl.BlockSpec((d, n), lambda i: (0, 0)),
            pl.BlockSpec((1, n), lambda i: (0, 0)),
        ],
        out_specs=[pl.BlockSpec((tm, c), lambda i: (i, 0)) for c, _ in outs],
        out_shape=[jax.ShapeDtypeStruct((m, c), dt) for c, dt in outs],
        compiler_params=_params(("arbitrary",)),
        name="norm_proj",
    )(x2, g.reshape(1, d), w, colscale.reshape(1, n))


def _mix_ffn_kernel(*refs, n_parts, ff_chunk, final):
    x_ref = refs[0]
    part_refs = refs[1:1 + n_parts]
    wo_ref, g_ref, wg_ref, wu_ref, wd_ref = refs[1 + n_parts:6 + n_parts]
    fg_ref = refs[6 + n_parts] if final else None
    o_ref = refs[-1]

    y = jnp.concatenate([p_ref[...] for p_ref in part_refs], axis=1)
    x = x_ref[...] + _dot(y, wo_ref[...])
    h = _rms(x, g_ref[...], RMS_EPS).astype(BF16)
    d_ff = wg_ref.shape[1]
    acc = x
    for c in range(0, d_ff, ff_chunk):
        gate = _dot(h, wg_ref[:, c:c + ff_chunk])
        up = _dot(h, wu_ref[:, c:c + ff_chunk])
        a = (gate * jax.nn.sigmoid(gate) * up).astype(BF16)
        acc = acc + _dot(a, wd_ref[c:c + ff_chunk, :])
    if final:
        acc = _rms(acc, fg_ref[...], RMS_EPS)
    o_ref[...] = acc


def _mix_ffn(x2, parts, w_out, g, wg, wu, wd, final_g=None, *, tm=512, ff_chunk=256):
    m, d = x2.shape
    d_ff = wg.shape[1]
    final = final_g is not None
    const = lambda i: (0, 0)
    in_specs = [pl.BlockSpec((tm, d), lambda i: (i, 0))]
    in_specs += [pl.BlockSpec((tm, p.shape[1]), lambda i: (i, 0)) for p in parts]
    in_specs += [
        pl.BlockSpec((d, d), const),
        pl.BlockSpec((1, d), const),
        pl.BlockSpec((d, d_ff), const),
        pl.BlockSpec((d, d_ff), const),
        pl.BlockSpec((d_ff, d), const),
    ]
    args = [x2, *parts, w_out, g.reshape(1, d), wg, wu, wd]
    if final:
        in_specs.append(pl.BlockSpec((1, d), const))
        args.append(final_g.reshape(1, d))
    return pl.pallas_call(
        functools.partial(_mix_ffn_kernel, n_parts=len(parts), ff_chunk=ff_chunk, final=final),
        grid=(m // tm,),
        in_specs=in_specs,
        out_specs=pl.BlockSpec((tm, d), lambda i: (i, 0)),
        out_shape=jax.ShapeDtypeStruct((m, d), F32),
        compiler_params=_params(("arbitrary",)),
        name="mix_ffn",
    )(*args)


def _branch_geometry(t, window, dilation, q_blk):
    seq = t // dilation
    half = window // (2 * dilation)
    q_blk = min(q_blk, seq)
    k_win = min(q_blk + 2 * half, seq)
    assert seq % q_blk == 0 and q_blk % HEAD_DIM == 0
    assert q_blk == 2 * half or seq == q_blk
    return seq, half, q_blk, k_win


def _dilated_kernel(slope_ref, q_ref, k_ref, v_ref, y_ref, so_ref, sl_ref, *bias_refs, q_blk, group):
    hp = pl.program_id(1)
    t = q_ref.shape[1]
    order = sorted(range(len(DILATED_PAIRS)), key=lambda i: -DILATED_PAIRS[i][1])

    for step, bi in enumerate(order):
        window, d = DILATED_PAIRS[bi]
        seq, half, qb_rows, k_win = _branch_geometry(t, window, d, q_blk)
        bias_ref = bias_refs[bi]
        first, last = step == 0, step == len(order) - 1

        row = lax.broadcasted_iota(jnp.int32, (2 * qb_rows, k_win), 0)
        col = lax.broadcasted_iota(jnp.int32, (2 * qb_rows, k_win), 1)
        second = row >= qb_rows
        rel = col - jnp.where(second, row - qb_rows, row)
        slope2 = jnp.where(second, slope_ref[2 * hp + 1], slope_ref[2 * hp]) * (LOG2E * d)
        for si in range(bias_ref.shape[0]):
            dist = jnp.abs(rel - si * half)
            bias_ref[si] = jnp.where(dist <= half, -slope2 * dist.astype(F32), NEG_BIG)

        lane = lax.broadcasted_iota(jnp.int32, (qb_rows, LANES), 1)
        first_head = lane < HEAD_DIM
        n_blk = seq // qb_rows

        def tile_rows(idx, d=d, half=half, seq=seq, qb_rows=qb_rows, k_win=k_win, n_blk=n_blk):
            r = lax.div(idx, jnp.int32(n_blk))
            q0 = lax.rem(idx, jnp.int32(n_blk)) * qb_rows
            ws = jnp.clip(q0 - half, 0, seq - k_win)
            if d == 1:
                rows_q = pl.ds(pl.multiple_of(q0, qb_rows), qb_rows)
                rows_k = pl.ds(pl.multiple_of(ws, HEAD_DIM), k_win)
            else:
                rows_q = pl.ds(r + d * q0, qb_rows, stride=d)
                rows_k = pl.ds(r + d * ws, k_win, stride=d)
            return rows_q, rows_k, lax.div(q0 - ws, jnp.int32(half))

        def group_body(g, carry, d=d, qb_rows=qb_rows, bias_ref=bias_ref, first=first, last=last,
                       first_head=first_head, tile_rows=tile_rows):
            tiles = []
            for i in range(group):
                rows_q, rows_k, si = tile_rows(g * group + i)
                q = q_ref[0, rows_q, :].astype(BF16)
                zero = jnp.zeros_like(q)
                qq = jnp.concatenate([jnp.where(first_head, q, zero), jnp.where(first_head, zero, q)], axis=0)
                kw = k_ref[0, rows_k, :].astype(BF16)
                s = _dot_nt(qq, kw) + bias_ref[si]
                tiles.append((rows_q, rows_k, s))
            soft = []
            for rows_q, rows_k, s in tiles:
                mx = jnp.max(s, axis=-1, keepdims=True)
                p = jnp.exp2(s - mx)
                den = jnp.sum(p, axis=-1, keepdims=True)
                soft.append((rows_q, rows_k, p.astype(BF16), den, mx + jnp.log2(den)))
            for rows_q, rows_k, p, den, lse2 in soft:
                vw = v_ref[0, rows_k, :].astype(BF16)
                o2 = _dot(p, vw) / den
                o = jnp.where(first_head, o2[0:qb_rows], o2[qb_rows:2 * qb_rows])
                lse = jnp.where(first_head, lse2[0:qb_rows], lse2[qb_rows:2 * qb_rows])
                if not first:
                    o_prev = so_ref[rows_q, :]
                    l_prev = sl_ref[rows_q, :]
                    top = jnp.maximum(l_prev, lse)
                    w_prev = jnp.exp2(l_prev - top)
                    w_cur = jnp.exp2(lse - top)
                    tot = w_prev + w_cur
                    o = (o_prev * w_prev + o * w_cur) / tot
                    lse = top + jnp.log2(tot)
                if last:
                    y_ref[0, rows_q, :] = o.astype(y_ref.dtype)
                else:
                    so_ref[rows_q, :] = o
                    sl_ref[rows_q, :] = lse
            return carry

        n_total = d * n_blk
        assert n_total % group == 0
        lax.fori_loop(0, n_total // group, group_body, 0)


def _dilated_mixture(slopes, qkv_a, *, q_blk=128, group=4):
    b, t, _ = qkv_a.shape
    assert DILATED_PAIRS[0][1] == 1
    ab = W_A // LANES
    seq_blk = (1, t, LANES)
    bias_scratch = []
    for window, d in DILATED_PAIRS:
        seq, half, qb_rows, k_win = _branch_geometry(t, window, d, q_blk)
        n_shift = 1 if seq == qb_rows else 3
        bias_scratch.append(pltpu.VMEM((n_shift, 2 * qb_rows, k_win), F32))
    return pl.pallas_call(
        functools.partial(_dilated_kernel, q_blk=q_blk, group=group),
        grid=(b, ab),
        in_specs=[
            pl.BlockSpec(memory_space=pltpu.SMEM),
            pl.BlockSpec(seq_blk, lambda bi, hp: (bi, 0, hp)),
            pl.BlockSpec(seq_blk, lambda bi, hp: (bi, 0, ab + hp)),
            pl.BlockSpec(seq_blk, lambda bi, hp: (bi, 0, 2 * ab + hp)),
        ],
        out_specs=pl.BlockSpec(seq_blk, lambda bi, hp: (bi, 0, hp)),
        out_shape=jax.ShapeDtypeStruct((b, t, W_A), BF16),
        scratch_shapes=[pltpu.VMEM((t, LANES), F32), pltpu.VMEM((t, LANES), F32)] + bias_scratch,
        compiler_params=_params(("arbitrary",) * 2),
        name="dilated_mixture",
    )(slopes, qkv_a, qkv_a, qkv_a)


def _natten_kernel(q_ref, k_ref, v_ref, tbl_ref, y_ref, *, rows, group):
    n_keys = NA_ROWS * GRID_W
    lane = lax.broadcasted_iota(jnp.int32, (GRID_W, LANES), 1)
    first_head = lane < HEAD_DIM

    def group_body(g, carry):
        tiles = []
        for i in range(group):
            r = g * group + i
            rs = jnp.clip(r - NA_ROWS // 2, 0, rows - NA_ROWS)
            q0 = pl.multiple_of(r * GRID_W, GRID_W)
            k0 = pl.multiple_of(rs * GRID_W, GRID_W)
            q = q_ref[0, pl.ds(q0, GRID_W), :]
            zero = jnp.zeros_like(q)
            qq = jnp.concatenate([jnp.where(first_head, q, zero), jnp.where(first_head, zero, q)], axis=0)
            s = _dot_nt(qq, k_ref[0, pl.ds(k0, n_keys), :]) + tbl_ref[0, r - rs]
            tiles.append((q0, k0, s))
        soft = []
        for q0, k0, s in tiles:
            mx = jnp.max(s, axis=-1, keepdims=True)
            p = jnp.exp2(s - mx)
            soft.append((q0, k0, p.astype(BF16), jnp.sum(p, axis=-1, keepdims=True)))
        for q0, k0, p, den in soft:
            o2 = _dot(p, v_ref[0, pl.ds(k0, n_keys), :]) / den
            o = jnp.where(first_head, o2[0:GRID_W], o2[GRID_W:2 * GRID_W])
            y_ref[0, pl.ds(q0, GRID_W), :] = o.astype(y_ref.dtype)
        return carry

    lax.fori_loop(0, rows // group, group_body, 0)


def _natten_bias_table(rpb):
    kh = NA_ROWS
    n_h = rpb.shape[0]
    c = np.arange(GRID_W)[:, None]
    kc = np.arange(GRID_W)[None, :]
    wc = np.clip(c - NA_COLS // 2, 0, GRID_W - NA_COLS)
    col_valid = (kc >= wc) & (kc < wc + NA_COLS)
    col_idx = kc - c + NA_COLS - 1
    pick = (col_idx[None] == np.arange(2 * NA_COLS - 1)[:, None, None]) & col_valid[None]
    band = jnp.einsum("hrm,mck->hrck", rpb.astype(F32) * LOG2E, jnp.asarray(pick, F32),
                      precision=lax.Precision.HIGHEST)
    band = jnp.where(col_valid[None, None], band, NEG_BIG)
    tbl = jnp.stack([band[:, kh - 1 - dl:2 * kh - 1 - dl] for dl in range(kh)], axis=1)
    tbl = tbl.reshape(n_h // 2, 2, kh, kh, GRID_W, GRID_W)
    tbl = tbl.transpose(0, 2, 1, 4, 3, 5)
    return tbl.reshape(n_h // 2, kh, 2 * GRID_W, kh * GRID_W)


def _natten(qkv_b, rpb, *, group=4):
    b, t, _ = qkv_b.shape
    rows = t // GRID_W
    assert rows >= NA_ROWS and t % GRID_W == 0 and rows % group == 0
    tbl = _natten_bias_table(rpb)
    bb = W_B // LANES
    seq_blk = (1, t, LANES)
    return pl.pallas_call(
        functools.partial(_natten_kernel, rows=rows, group=group),
        grid=(b, bb),
        in_specs=[
            pl.BlockSpec(seq_blk, lambda bi, hp: (bi, 0, hp)),
            pl.BlockSpec(seq_blk, lambda bi, hp: (bi, 0, bb + hp)),
            pl.BlockSpec(seq_blk, lambda bi, hp: (bi, 0, 2 * bb + hp)),
            pl.BlockSpec((1, NA_ROWS, 2 * GRID_W, NA_ROWS * GRID_W), lambda bi, hp: (hp, 0, 0, 0)),
        ],
        out_specs=pl.BlockSpec(seq_blk, lambda bi, hp: (bi, 0, hp)),
        out_shape=jax.ShapeDtypeStruct((b, t, W_B), BF16),
        compiler_params=_params(("arbitrary",) * 2),
        name="natten",
    )(qkv_b, qkv_b, qkv_b, tbl)


def _diff_attn_kernel(slope_ref, q_ref, k_ref, v_ref, lq1_ref, lk1_ref, lq2_ref, lk2_ref,
                      subln_ref, y_ref, qq_ref, rel_ref, vt_ref, s_ref, mx_ref,
                      m_ref, l_ref, acc_ref, *, tq, tk, lam_init):
    h = pl.program_id(1)
    qi = pl.program_id(2)
    n_kv = k_ref.shape[1] // tk
    slope2 = slope_ref[h] * LOG2E
    i0 = qi * tq
    diag = i0 // tk

    @pl.when(qi == 0)
    def _per_head_setup():
        key = lax.broadcasted_iota(jnp.int32, (tk, 2 * tq), 0)
        qry = lax.broadcasted_iota(jnp.int32, (tk, 2 * tq), 1)
        qry = jnp.where(qry >= tq, qry - tq, qry)
        rel = (key - qry).astype(F32) * slope2
        rel_ref[0] = rel
        rel_ref[1] = -rel
        for c in range(n_kv):
            vt_ref[c] = v_ref[0, c * tk:(c + 1) * tk, :].astype(F32).T.astype(BF16)

    q = q_ref[0]
    lane = lax.broadcasted_iota(jnp.int32, (tq, LANES), 1)
    zero = jnp.zeros_like(q)
    qq_ref[0:tq, :] = jnp.where(lane < HEAD_DIM, q, zero)
    qq_ref[tq:2 * tq, :] = jnp.where(lane < HEAD_DIM, zero, q)

    m_ref[...] = jnp.full(m_ref.shape, NEG_BIG, F32)
    l_ref[...] = jnp.zeros(l_ref.shape, F32)
    acc_ref[...] = jnp.zeros(acc_ref.shape, F32)

    def block_offset(j):
        return (j * tk - i0).astype(F32) * slope2

    def produce(j, slot, overlapping):
        j0 = pl.multiple_of(j * tk, tk)
        s = _dot_nt(k_ref[0, pl.ds(j0, tk), :], qq_ref[...])
        if overlapping:
            s = s - jnp.abs(rel_ref[0] + block_offset(j))
        else:
            s = s + rel_ref[(j > diag).astype(jnp.int32)]
        s_ref[slot] = s
        mx_ref[slot] = jnp.max(s, axis=0, keepdims=True)

    def consume(j, slot, overlapping):
        if overlapping:
            shift = jnp.zeros((), F32)
        else:
            off = block_offset(j)
            shift = jnp.where(j > diag, -off, off)
        m_prev = m_ref[...]
        m_new = jnp.maximum(m_prev, mx_ref[slot] + shift)
        alpha = jnp.exp2(m_prev - m_new)
        p = jnp.exp2(s_ref[slot] - (m_new - shift))
        l_ref[...] = alpha * l_ref[...] + jnp.sum(p, axis=0, keepdims=True)
        acc_ref[...] = alpha * acc_ref[...] + _dot(vt_ref[j], p.astype(BF16))
        m_ref[...] = m_new

    def block_at(t):
        if t == 0:
            return diag
        return (t - 1) + ((t - 1) >= diag).astype(jnp.int32)

    produce(block_at(0), 0, True)
    for t in range(n_kv):
        if t + 1 < n_kv:
            produce(block_at(t + 1), (t + 1) % 2, False)
        consume(block_at(t), t % 2, t == 0)

    lam = (jnp.exp(jnp.sum(lq1_ref[...] * lk1_ref[...], axis=-1, keepdims=True))
           - jnp.exp(jnp.sum(lq2_ref[...] * lk2_ref[...], axis=-1, keepdims=True)) + lam_init)
    o = acc_ref[...] / l_ref[...]
    y = (o[:, 0:tq] - lam * o[:, tq:2 * tq]).T
    y = _rms(y, subln_ref[...], SUBLN_EPS) * (1.0 - lam_init)
    y_ref[0] = y.astype(y_ref.dtype)


def _diff_attn(slopes, qkv, lq1, lk1, lq2, lk2, subln, lam_init, *, tq=256, tk=512):
    b, t, n_cols = qkv.shape
    nh = n_cols // (3 * LANES)
    assert t % tk == 0 and tk % tq == 0
    vec = lambda a: a.reshape(1, -1).astype(F32)
    small = lambda n: pl.BlockSpec((1, n), lambda bi, h, qi: (0, 0))
    return pl.pallas_call(
        functools.partial(_diff_attn_kernel, tq=tq, tk=tk, lam_init=lam_init),
        grid=(b, nh, t // tq),
        in_specs=[
            pl.BlockSpec(memory_space=pltpu.SMEM),
            pl.BlockSpec((1, tq, LANES), lambda bi, h, qi: (bi, qi, h)),
            pl.BlockSpec((1, t, LANES), lambda bi, h, qi: (bi, 0, nh + h)),
            pl.BlockSpec((1, t, LANES), lambda bi, h, qi: (bi, 0, 2 * nh + h)),
            small(HEAD_DIM), small(HEAD_DIM), small(HEAD_DIM), small(HEAD_DIM),
            small(2 * HEAD_DIM),
        ],
        out_specs=pl.BlockSpec((1, tq, LANES), lambda bi, h, qi: (bi, qi, h)),
        out_shape=jax.ShapeDtypeStruct((b, t, nh * LANES), BF16),
        scratch_shapes=[
            pltpu.VMEM((2 * tq, LANES), BF16),
            pltpu.VMEM((2, tk, 2 * tq), F32),
            pltpu.VMEM((t // tk, LANES, tk), BF16),
            pltpu.VMEM((2, tk, 2 * tq), F32),
            pltpu.VMEM((2, 1, 2 * tq), F32),
            pltpu.VMEM((1, 2 * tq), F32),
            pltpu.VMEM((1, 2 * tq), F32),
            pltpu.VMEM((LANES, 2 * tq), F32),
        ],
        compiler_params=_params(("arbitrary",) * 3),
        name="diff_attn",
    )(slopes, qkv, qkv, qkv, vec(lq1), vec(lk1), vec(lq2), vec(lk2), vec(subln))


def _alibi_slopes(n):
    return jnp.exp2(-8.0 * jnp.arange(1, n + 1, dtype=F32) / n)


def _lambda_init(layer):
    return 0.8 - 0.6 * math.exp(-0.3 * layer)


def kernel(x, a0_norm, a0_w_in, a0_w_out, a0_rpb, f0_norm, f0_w_gate, f0_w_up, f0_w_down,
           a1_norm, a1_w_qkv, a1_w_out, a1_lam_q1, a1_lam_k1, a1_lam_q2, a1_lam_k2, a1_subln,
           f1_norm, f1_w_gate, f1_w_up, f1_w_down, final_norm):
    b, t, d = x.shape
    x2 = x.reshape(b * t, d)
    qscale = HEAD_DIM ** -0.5 * LOG2E
    bf = lambda w: w.astype(BF16)

    cs0 = np.ones((3 * W_A + 3 * W_B,), np.float32)
    cs0[0:W_A] = qscale
    cs0[3 * W_A:3 * W_A + W_B] = qscale
    qkv_a, qkv_b = _norm_proj(x2, a0_norm, bf(a0_w_in), jnp.asarray(cs0),
                              ((3 * W_A, F32), (3 * W_B, BF16)))
    ya = _dilated_mixture(_alibi_slopes(N_HEADS_A), qkv_a.reshape(b, t, 3 * W_A)).reshape(b * t, W_A)
    yb = _natten(qkv_b.reshape(b, t, 3 * W_B), a0_rpb).reshape(b * t, W_B)
    x2 = _mix_ffn(x2, [ya, yb], bf(a0_w_out), f0_norm, bf(f0_w_gate), bf(f0_w_up), bf(f0_w_down))

    wc = a1_w_qkv.shape[1] // 3
    cs1 = np.ones((3 * wc,), np.float32)
    cs1[0:wc] = qscale
    (qkv1,) = _norm_proj(x2, a1_norm, bf(a1_w_qkv), jnp.asarray(cs1), ((3 * wc, BF16),))
    qkv1 = qkv1.reshape(b, t, 3 * wc)
    n_heads_c = wc // (2 * HEAD_DIM)
    y = _diff_attn(_alibi_slopes(n_heads_c), qkv1, a1_lam_q1, a1_lam_k1, a1_lam_q2, a1_lam_k2,
                   a1_subln, _lambda_init(1)).reshape(b * t, wc)
    out = _mix_ffn(x2, [y], bf(a1_w_out), f1_norm, bf(f1_w_gate), bf(f1_w_up), bf(f1_w_down),
                   final_norm)
    return out.reshape(b, t, d)
```

```python
import functools
import math

import numpy as np
import jax
import jax.numpy as jnp
from jax import lax
from jax.experimental import pallas as pl
from jax.experimental.pallas import tpu as pltpu

HEAD_DIM = 64
N_HEADS_A = 8
N_HEADS_B = 8
W_A = N_HEADS_A * HEAD_DIM
W_B = N_HEADS_B * HEAD_DIM
DILATED_PAIRS = ((128, 1), (512, 4), (2048, 16))
GRID_W = 64
NA_ROWS = 8
NA_COLS = 16
RMS_EPS = 1e-6
SUBLN_EPS = 1e-5
LOG2E = math.log2(math.e)

LANES = 128
DEN_ROWS = 16
V7X_VMEM_LIMIT_BYTES = 56 * 1024 * 1024

NEG_BIG = -1e30

BF16 = jnp.bfloat16
F32 = jnp.float32


def _params(semantics):
    return pltpu.CompilerParams(dimension_semantics=semantics,
                                vmem_limit_bytes=V7X_VMEM_LIMIT_BYTES)


def _rms(x, g, eps):
    ms = jnp.mean(x * x, axis=-1, keepdims=True)
    return (x * lax.rsqrt(ms + eps)) * g


def _dot_nt(a, b):
    return lax.dot_general(a, b, (((1,), (1,)), ((), ())), preferred_element_type=F32)


def _dot(a, b):
    return jnp.dot(a, b, preferred_element_type=F32)


def _norm_proj_kernel(x_ref, g_ref, w_ref, cs_ref, *o_refs, n_chunk):
    h = _rms(x_ref[...], g_ref[...], RMS_EPS).astype(BF16)
    col = 0
    for o_ref in o_refs:
        for c in range(0, o_ref.shape[1], n_chunk):
            acc = _dot(h, w_ref[:, col + c:col + c + n_chunk])
            o_ref[:, c:c + n_chunk] = (acc * cs_ref[:, col + c:col + c + n_chunk]).astype(o_ref.dtype)
        col += o_ref.shape[1]


def _norm_proj(x2, g, w, colscale, outs, *, tm=512, n_chunk=512):
    m, d = x2.shape
    n = w.shape[1]
    assert sum(c for c, _ in outs) == n and all(c % n_chunk == 0 for c, _ in outs)
    return pl.pallas_call(
        functools.partial(_norm_proj_kernel, n_chunk=n_chunk),
        grid=(m // tm,),
        in_specs=[
            pl.BlockSpec((tm, d), lambda i: (i, 0)),
            pl.BlockSpec((1, d), lambda i: (0, 0)),
            pl.BlockSpec((d, n), lambda i: (0, 0)),
            pl.BlockSpec((1, n), lambda i: (0, 0)),
        ],
        out_specs=[pl.BlockSpec((tm, c), lambda i: (i, 0)) for c, _ in outs],
        out_shape=[jax.ShapeDtypeStruct((m, c), dt) for c, dt in outs],
        compiler_params=_params(("arbitrary",)),
        name="norm_proj",
    )(x2, g.reshape(1, d), w, colscale.reshape(1, n))


def _mix_ffn_kernel(*refs, n_parts, ff_chunk, final):
    x_ref = refs[0]
    part_refs = refs[1:1 + n_parts]
    wo_ref, g_ref, wg_ref, wu_ref, wd_ref = refs[1 + n_parts:6 + n_parts]
    fg_ref = refs[6 + n_parts] if final else None
    o_ref = refs[-1]

    y = jnp.concatenate([p_ref[...] for p_ref in part_refs], axis=1)
    x = x_ref[...] + _dot(y, wo_ref[...])
    h = _rms(x, g_ref[...], RMS_EPS).astype(BF16)
    d_ff = wg_ref.shape[1]
    acc = x
    for c in range(0, d_ff, ff_chunk):
        gate = _dot(h, wg_ref[:, c:c + ff_chunk])
        up = _dot(h, wu_ref[:, c:c + ff_chunk])
        a = (gate * jax.nn.sigmoid(gate) * up).astype(BF16)
        acc = acc + _dot(a, wd_ref[c:c + ff_chunk, :])
    if final:
        acc = _rms(acc, fg_ref[...], RMS_EPS)
    o_ref[...] = acc


def _mix_ffn(x2, parts, w_out, g, wg, wu, wd, final_g=None, *, tm=512, ff_chunk=256):
    m, d = x2.shape
    d_ff = wg.shape[1]
    final = final_g is not None
    const = lambda i: (0, 0)
    in_specs = [pl.BlockSpec((tm, d), lambda i: (i, 0))]
    in_specs += [pl.BlockSpec((tm, p.shape[1]), lambda i: (i, 0)) for p in parts]
    in_specs += [
        pl.BlockSpec((d, d), const),
        pl.BlockSpec((1, d), const),
        pl.BlockSpec((d, d_ff), const),
        pl.BlockSpec((d, d_ff), const),
        pl.BlockSpec((d_ff, d), const),
    ]
    args = [x2, *parts, w_out, g.reshape(1, d), wg, wu, wd]
    if final:
        in_specs.append(pl.BlockSpec((1, d), const))
        args.append(final_g.reshape(1, d))
    return pl.pallas_call(
        functools.partial(_mix_ffn_kernel, n_parts=len(parts), ff_chunk=ff_chunk, final=final),
        grid=(m // tm,),
        in_specs=in_specs,
        out_specs=pl.BlockSpec((tm, d), lambda i: (i, 0)),
        out_shape=jax.ShapeDtypeStruct((m, d), F32),
        compiler_params=_params(("arbitrary",)),
        name="mix_ffn",
    )(*args)


def _branch_geometry(t, window, dilation, q_blk):
    seq = t // dilation
    half = window // (2 * dilation)
    q_blk = min(q_blk, seq)
    k_win = min(q_blk + 2 * half, seq)
    assert seq % q_blk == 0 and q_blk % HEAD_DIM == 0
    assert q_blk == 2 * half or seq == q_blk
    return seq, half, q_blk, k_win


def _dilated_kernel(slope_ref, q_ref, k_ref, v_ref, y_ref, so_ref, sl_ref, *bias_refs, q_blk, group):
    hp = pl.program_id(1)
    t = q_ref.shape[1]
    order = sorted(range(len(DILATED_PAIRS)), key=lambda i: -DILATED_PAIRS[i][1])

    for step, bi in enumerate(order):
        window, d = DILATED_PAIRS[bi]
        seq, half, qb_rows, k_win = _branch_geometry(t, window, d, q_blk)
        bias_ref = bias_refs[bi]
        first, last = step == 0, step == len(order) - 1

        row = lax.broadcasted_iota(jnp.int32, (2 * qb_rows, k_win), 0)
        col = lax.broadcasted_iota(jnp.int32, (2 * qb_rows, k_win), 1)
        second = row >= qb_rows
        rel = col - jnp.where(second, row - qb_rows, row)
        slope2 = jnp.where(second, slope_ref[2 * hp + 1], slope_ref[2 * hp]) * (LOG2E * d)
        for si in range(bias_ref.shape[0]):
            dist = jnp.abs(rel - si * half)
            bias_ref[si] = jnp.where(dist <= half, -slope2 * dist.astype(F32), NEG_BIG)

        lane = lax.broadcasted_iota(jnp.int32, (qb_rows, LANES), 1)
        first_head = lane < HEAD_DIM
        n_blk = seq // qb_rows

        def tile_rows(idx, d=d, half=half, seq=seq, qb_rows=qb_rows, k_win=k_win, n_blk=n_blk):
            r = lax.div(idx, jnp.int32(n_blk))
            q0 = lax.rem(idx, jnp.int32(n_blk)) * qb_rows
            ws = jnp.clip(q0 - half, 0, seq - k_win)
            if d == 1:
                rows_q = pl.ds(pl.multiple_of(q0, qb_rows), qb_rows)
                rows_k = pl.ds(pl.multiple_of(ws, HEAD_DIM), k_win)
            else:
                rows_q = pl.ds(r + d * q0, qb_rows, stride=d)
                rows_k = pl.ds(r + d * ws, k_win, stride=d)
            return rows_q, rows_k, lax.div(q0 - ws, jnp.int32(half))

        def group_body(g, carry, d=d, qb_rows=qb_rows, bias_ref=bias_ref, first=first, last=last,
                       first_head=first_head, tile_rows=tile_rows):
            tiles = []
            for i in range(group):
                rows_q, rows_k, si = tile_rows(g * group + i)
                q = q_ref[0, rows_q, :].astype(BF16)
                zero = jnp.zeros_like(q)
                qq = jnp.concatenate([jnp.where(first_head, q, zero), jnp.where(first_head, zero, q)], axis=0)
                kw = k_ref[0, rows_k, :].astype(BF16)
                s = _dot_nt(qq, kw) + bias_ref[si]
                tiles.append((rows_q, rows_k, s))
            soft = []
            for rows_q, rows_k, s in tiles:
                mx = jnp.max(s, axis=-1, keepdims=True)
                p = jnp.exp2(s - mx)
                den = jnp.sum(p, axis=-1, keepdims=True)
                soft.append((rows_q, rows_k, p.astype(BF16), den, mx + jnp.log2(den)))
            for rows_q, rows_k, p, den, lse2 in soft:
                vw = v_ref[0, rows_k, :].astype(BF16)
                o2 = _dot(p, vw) / den
                o = jnp.where(first_head, o2[0:qb_rows], o2[qb_rows:2 * qb_rows])
                lse = jnp.where(first_head, lse2[0:qb_rows], lse2[qb_rows:2 * qb_rows])
                if not first:
                    o_prev = so_ref[rows_q, :]
                    l_prev = sl_ref[rows_q, :]
                    top = jnp.maximum(l_prev, lse)
                    w_prev = jnp.exp2(l_prev - top)
                    w_cur = jnp.exp2(lse - top)
                    tot = w_prev + w_cur
                    o = (o_prev * w_prev + o * w_cur) / tot
                    lse = top + jnp.log2(tot)
                if last:
                    y_ref[0, rows_q, :] = o.astype(y_ref.dtype)
                else:
                    so_ref[rows_q, :] = o
                    sl_ref[rows_q, :] = lse
            return carry

        n_total = d * n_blk
        assert n_total % group == 0
        lax.fori_loop(0, n_total // group, group_body, 0)


def _dilated_mixture(slopes, qkv_a, *, q_blk=128, group=4):
    b, t, _ = qkv_a.shape
    assert DILATED_PAIRS[0][1] == 1
    ab = W_A // LANES
    seq_blk = (1, t, LANES)
    bias_scratch = []
    for window, d in DILATED_PAIRS:
        seq, half, qb_rows, k_win = _branch_geometry(t, window, d, q_blk)
        n_shift = 1 if seq == qb_rows else 3
        bias_scratch.append(pltpu.VMEM((n_shift, 2 * qb_rows, k_win), F32))
    return pl.pallas_call(
        functools.partial(_dilated_kernel, q_blk=q_blk, group=group),
        grid=(b, ab),
        in_specs=[
            pl.BlockSpec(memory_space=pltpu.SMEM),
            pl.BlockSpec(seq_blk, lambda bi, hp: (bi, 0, hp)),
            pl.BlockSpec(seq_blk, lambda bi, hp: (bi, 0, ab + hp)),
            pl.BlockSpec(seq_blk, lambda bi, hp: (bi, 0, 2 * ab + hp)),
        ],
        out_specs=pl.BlockSpec(seq_blk, lambda bi, hp: (bi, 0, hp)),
        out_shape=jax.ShapeDtypeStruct((b, t, W_A), BF16),
        scratch_shapes=[pltpu.VMEM((t, LANES), F32), pltpu.VMEM((t, LANES), F32)] + bias_scratch,
        compiler_params=_params(("arbitrary",) * 2),
        name="dilated_mixture",
    )(slopes, qkv_a, qkv_a, qkv_a)


def _natten_kernel(q_ref, k_ref, v_ref, tbl_ref, y_ref, *, rows, group):
    n_keys = NA_ROWS * GRID_W
    lane = lax.broadcasted_iota(jnp.int32, (GRID_W, LANES), 1)
    first_head = lane < HEAD_DIM

    def group_body(g, carry):
        tiles = []
        for i in range(group):
            r = g * group + i
            rs = jnp.clip(r - NA_ROWS // 2, 0, rows - NA_ROWS)
            q0 = pl.multiple_of(r * GRID_W, GRID_W)
            k0 = pl.multiple_of(rs * GRID_W, GRID_W)
            q = q_ref[0, pl.ds(q0, GRID_W), :]
            zero = jnp.zeros_like(q)
            qq = jnp.concatenate([jnp.where(first_head, q, zero), jnp.where(first_head, zero, q)], axis=0)
            s = _dot_nt(qq, k_ref[0, pl.ds(k0, n_keys), :]) + tbl_ref[0, r - rs]
            tiles.append((q0, k0, s))
        soft = []
        for q0, k0, s in tiles:
            mx = jnp.max(s, axis=-1, keepdims=True)
            p = jnp.exp2(s - mx)
            soft.append((q0, k0, p.astype(BF16), jnp.sum(p, axis=-1, keepdims=True)))
        for q0, k0, p, den in soft:
            o2 = _dot(p, v_ref[0, pl.ds(k0, n_keys), :]) / den
            o = jnp.where(first_head, o2[0:GRID_W], o2[GRID_W:2 * GRID_W])
            y_ref[0, pl.ds(q0, GRID_W), :] = o.astype(y_ref.dtype)
        return carry

    lax.fori_loop(0, rows // group, group_body, 0)


def _natten_bias_table(rpb):
    kh = NA_ROWS
    n_h = rpb.shape[0]
    c = np.arange(GRID_W)[:, None]
    kc = np.arange(GRID_W)[None, :]
    wc = np.clip(c - NA_COLS // 2, 0, GRID_W - NA_COLS)
    col_valid = (kc >= wc) & (kc < wc + NA_COLS)
    col_idx = kc - c + NA_COLS - 1
    pick = (col_idx[None] == np.arange(2 * NA_COLS - 1)[:, None, None]) & col_valid[None]
    band = jnp.einsum("hrm,mck->hrck", rpb.astype(F32) * LOG2E, jnp.asarray(pick, F32),
                      precision=lax.Precision.HIGHEST)
    band = jnp.where(col_valid[None, None], band, NEG_BIG)
    tbl = jnp.stack([band[:, kh - 1 - dl:2 * kh - 1 - dl] for dl in range(kh)], axis=1)
    tbl = tbl.reshape(n_h // 2, 2, kh, kh, GRID_W, GRID_W)
    tbl = tbl.transpose(0, 2, 1, 4, 3, 5)
    return tbl.reshape(n_h // 2, kh, 2 * GRID_W, kh * GRID_W)


def _natten(qkv_b, rpb, *, group=4):
    b, t, _ = qkv_b.shape
    rows = t // GRID_W
    assert rows >= NA_ROWS and t % GRID_W == 0 and rows % group == 0
    tbl = _natten_bias_table(rpb)
    bb = W_B // LANES
    seq_blk = (1, t, LANES)
    return pl.pallas_call(
        functools.partial(_natten_kernel, rows=rows, group=group),
        grid=(b, bb),
        in_specs=[
            pl.BlockSpec(seq_blk, lambda bi, hp: (bi, 0, hp)),
            pl.BlockSpec(seq_blk, lambda bi, hp: (bi, 0, bb + hp)),
            pl.BlockSpec(seq_blk, lambda bi, hp: (bi, 0, 2 * bb + hp)),
            pl.BlockSpec((1, NA_ROWS, 2 * GRID_W, NA_ROWS * GRID_W), lambda bi, hp: (hp, 0, 0, 0)),
        ],
        out_specs=pl.BlockSpec(seq_blk, lambda bi, hp: (bi, 0, hp)),
        out_shape=jax.ShapeDtypeStruct((b, t, W_B), BF16),
        compiler_params=_params(("arbitrary",) * 2),
        name="natten",
    )(qkv_b, qkv_b, qkv_b, tbl)


def _diff_attn_kernel(slope_ref, q_ref, k_ref, v_ref, lq1_ref, lk1_ref, lq2_ref, lk2_ref,
                      subln_ref, y_ref, qq_ref, rel_ref, vt_ref, s_ref, mx_ref,
                      m_ref, acc_ref, *, tq, tk, n_sub, lam_init):
    h = pl.program_id(1)
    qi = pl.program_id(2)
    n_kv = k_ref.shape[1] // tk
    slope2 = slope_ref[h] * LOG2E

    @pl.when(qi == 0)
    def _per_head_setup():
        key = lax.broadcasted_iota(jnp.int32, (tk, 2 * tq), 0)
        qry = lax.broadcasted_iota(jnp.int32, (tk, 2 * tq), 1)
        qry = jnp.where(qry >= tq, qry - tq, qry)
        rel = (key - qry).astype(F32) * slope2
        rel_ref[0] = rel
        rel_ref[1] = -rel
        ones_row = lax.broadcasted_iota(jnp.int32, (DEN_ROWS, tk), 0) == 0
        for c in range(n_kv):
            vt_ref[c, 0:LANES, :] = v_ref[0, c * tk:(c + 1) * tk, :].astype(F32).T.astype(BF16)
            vt_ref[c, LANES:LANES + DEN_ROWS, :] = jnp.where(ones_row, 1.0, 0.0).astype(BF16)

    lam = (jnp.exp(jnp.sum(lq1_ref[...] * lk1_ref[...], axis=-1, keepdims=True))
           - jnp.exp(jnp.sum(lq2_ref[...] * lk2_ref[...], axis=-1, keepdims=True)) + lam_init)

    def tile_fns(u):
        i0 = (qi * n_sub + u) * tq
        diag = i0 // tk

        def block_offset(j):
            return (j * tk - i0).astype(F32) * slope2

        def produce(j, slot, overlapping):
            j0 = pl.multiple_of(j * tk, tk)
            s = _dot_nt(k_ref[0, pl.ds(j0, tk), :], qq_ref[u])
            if overlapping:
                s = s - jnp.abs(rel_ref[0] + block_offset(j))
            else:
                s = s + rel_ref[(j > diag).astype(jnp.int32)]
            s_ref[u, slot] = s
            mx_ref[u, slot] = jnp.max(s, axis=0, keepdims=True)

        def consume(j, slot, overlapping):
            if overlapping:
                shift = jnp.zeros((), F32)
            else:
                off = block_offset(j)
                shift = jnp.where(j > diag, -off, off)
            m_prev = m_ref[u]
            m_new = jnp.maximum(m_prev, mx_ref[u, slot] + shift)
            alpha = jnp.exp2(m_prev - m_new)
            p = jnp.exp2((s_ref[u, slot] - (m_new - shift)).astype(BF16))
            acc_ref[u] = alpha * acc_ref[u] + _dot(vt_ref[j], p)
            m_ref[u] = m_new

        def block_at(t):
            if t == 0:
                return diag
            return (t - 1) + ((t - 1) >= diag).astype(jnp.int32)

        def start():
            q = q_ref[0, u * tq:(u + 1) * tq, :]
            lane = lax.broadcasted_iota(jnp.int32, (tq, LANES), 1)
            zero = jnp.zeros_like(q)
            qq_ref[u, 0:tq, :] = jnp.where(lane < HEAD_DIM, q, zero)
            qq_ref[u, tq:2 * tq, :] = jnp.where(lane < HEAD_DIM, zero, q)
            m_ref[u] = jnp.full(m_ref.shape[1:], NEG_BIG, F32)
            acc_ref[u] = jnp.zeros(acc_ref.shape[1:], F32)
            produce(block_at(0), 0, True)

        def middle():
            for t in range(n_kv):
                if t + 1 < n_kv:
                    produce(block_at(t + 1), (t + 1) % 2, False)
                consume(block_at(t), t % 2, t == 0)

        def finish():
            o = acc_ref[u, 0:LANES, :] / acc_ref[u, LANES:LANES + 1, :]
            y = (o[:, 0:tq] - lam * o[:, tq:2 * tq]).T
            y = _rms(y, subln_ref[...], SUBLN_EPS) * (1.0 - lam_init)
            y_ref[0, u * tq:(u + 1) * tq, :] = y.astype(y_ref.dtype)

        return start, middle, finish

    fns = [tile_fns(u) for u in range(n_sub)]
    fns[0][0]()
    for u in range(n_sub):
        fns[u][1]()
        if u + 1 < n_sub:
            fns[u + 1][0]()
        fns[u][2]()


def _diff_attn(slopes, qkv, lq1, lk1, lq2, lk2, subln, lam_init, *, tq=256, tk=512, n_sub=4):
    b, t, n_cols = qkv.shape
    nh = n_cols // (3 * LANES)
    assert t % tk == 0 and tk % tq == 0 and t % (n_sub * tq) == 0
    tqs = n_sub * tq
    vec = lambda a: a.reshape(1, -1).astype(F32)
    small = lambda n: pl.BlockSpec((1, n), lambda bi, h, qi: (0, 0))
    return pl.pallas_call(
        functools.partial(_diff_attn_kernel, tq=tq, tk=tk, n_sub=n_sub, lam_init=lam_init),
        grid=(b, nh, t // tqs),
        in_specs=[
            pl.BlockSpec(memory_space=pltpu.SMEM),
            pl.BlockSpec((1, tqs, LANES), lambda bi, h, qi: (bi, qi, h)),
            pl.BlockSpec((1, t, LANES), lambda bi, h, qi: (bi, 0, nh + h)),
            pl.BlockSpec((1, t, LANES), lambda bi, h, qi: (bi, 0, 2 * nh + h)),
            small(HEAD_DIM), small(HEAD_DIM), small(HEAD_DIM), small(HEAD_DIM),
            small(2 * HEAD_DIM),
        ],
        out_specs=pl.BlockSpec((1, tqs, LANES), lambda bi, h, qi: (bi, qi, h)),
        out_shape=jax.ShapeDtypeStruct((b, t, nh * LANES), BF16),
        scratch_shapes=[
            pltpu.VMEM((n_sub, 2 * tq, LANES), BF16),
            pltpu.VMEM((2, tk, 2 * tq), F32),
            pltpu.VMEM((t // tk, LANES + DEN_ROWS, tk), BF16),
            pltpu.VMEM((n_sub, 2, tk, 2 * tq), F32),
            pltpu.VMEM((n_sub, 2, 1, 2 * tq), F32),
            pltpu.VMEM((n_sub, 1, 2 * tq), F32),
            pltpu.VMEM((n_sub, LANES + DEN_ROWS, 2 * tq), F32),
        ],
        compiler_params=_params(("arbitrary",) * 3),
        name="diff_attn",
    )(slopes, qkv, qkv, qkv, vec(lq1), vec(lk1), vec(lq2), vec(lk2), vec(subln))


def _alibi_slopes(n):
    return jnp.exp2(-8.0 * jnp.arange(1, n + 1, dtype=F32) / n)


def _lambda_init(layer):
    return 0.8 - 0.6 * math.exp(-0.3 * layer)


def kernel(x, a0_norm, a0_w_in, a0_w_out, a0_rpb, f0_norm, f0_w_gate, f0_w_up, f0_w_down,
           a1_norm, a1_w_qkv, a1_w_out, a1_lam_q1, a1_lam_k1, a1_lam_q2, a1_lam_k2, a1_subln,
           f1_norm, f1_w_gate, f1_w_up, f1_w_down, final_norm):
    b, t, d = x.shape
    x2 = x.reshape(b * t, d)
    qscale = HEAD_DIM ** -0.5 * LOG2E
    bf = lambda w: w.astype(BF16)

    cs0 = np.ones((3 * W_A + 3 * W_B,), np.float32)
    cs0[0:W_A] = qscale
    cs0[3 * W_A:3 * W_A + W_B] = qscale
    qkv_a, qkv_b = _norm_proj(x2, a0_norm, bf(a0_w_in), jnp.asarray(cs0),
                              ((3 * W_A, F32), (3 * W_B, BF16)))
    ya = _dilated_mixture(_alibi_slopes(N_HEADS_A), qkv_a.reshape(b, t, 3 * W_A)).reshape(b * t, W_A)
    yb = _natten(qkv_b.reshape(b, t, 3 * W_B), a0_rpb).reshape(b * t, W_B)
    x2 = _mix_ffn(x2, [ya, yb], bf(a0_w_out), f0_norm, bf(f0_w_gate), bf(f0_w_up), bf(f0_w_down))

    wc = a1_w_qkv.shape[1] // 3
    cs1 = np.ones((3 * wc,), np.float32)
    cs1[0:wc] = qscale
    (qkv1,) = _norm_proj(x2, a1_norm, bf(a1_w_qkv), jnp.asarray(cs1), ((3 * wc, BF16),))
    qkv1 = qkv1.reshape(b, t, 3 * wc)
    n_heads_c = wc // (2 * HEAD_DIM)
    y = _diff_attn(_alibi_slopes(n_heads_c), qkv1, a1_lam_q1, a1_lam_k1, a1_lam_q2, a1_lam_k2,
                   a1_subln, _lambda_init(1)).reshape(b * t, wc)
    out = _mix_ffn(x2, [y], bf(a1_w_out), f1_norm, bf(f1_w_gate), bf(f1_w_up), bf(f1_w_down),
                   final_norm)
    return out.reshape(b, t, d)
```

```python
import functools
import math

import numpy as np
import jax
import jax.numpy as jnp
from jax import lax
from jax.experimental import pallas as pl
from jax.experimental.pallas import tpu as pltpu

HEAD_DIM = 64
N_HEADS_A = 8
N_HEADS_B = 8
W_A = N_HEADS_A * HEAD_DIM
W_B = N_HEADS_B * HEAD_DIM
DILATED_PAIRS = ((128, 1), (512, 4), (2048, 16))
GRID_W = 64
NA_ROWS = 8
NA_COLS = 16
RMS_EPS = 1e-6
SUBLN_EPS = 1e-5
LOG2E = math.log2(math.e)

LANES = 128
OUT_CHUNK = 512
DEN_ROWS = 16
V7X_VMEM_LIMIT_BYTES = 56 * 1024 * 1024

NEG_BIG = -1e30

BF16 = jnp.bfloat16
F32 = jnp.float32


def _params(semantics):
    return pltpu.CompilerParams(dimension_semantics=semantics,
                                vmem_limit_bytes=V7X_VMEM_LIMIT_BYTES)


def _rms(x, g, eps):
    ms = jnp.mean(x * x, axis=-1, keepdims=True)
    return (x * lax.rsqrt(ms + eps)) * g


def _dot_nt(a, b):
    return lax.dot_general(a, b, (((1,), (1,)), ((), ())), preferred_element_type=F32)


def _dot(a, b):
    return jnp.dot(a, b, preferred_element_type=F32)


def _norm_proj_kernel(x_ref, g_ref, w_ref, cs_ref, *o_refs, n_chunk):
    h = _rms(x_ref[...], g_ref[...], RMS_EPS).astype(BF16)
    col = 0
    for o_ref in o_refs:
        for c in range(0, o_ref.shape[1], n_chunk):
            acc = _dot(h, w_ref[:, col + c:col + c + n_chunk])
            o_ref[:, c:c + n_chunk] = (acc * cs_ref[:, col + c:col + c + n_chunk]).astype(o_ref.dtype)
        col += o_ref.shape[1]


def _norm_proj(x2, g, w, colscale, outs, *, tm=1024, n_chunk=512):
    m, d = x2.shape
    n = w.shape[1]
    assert sum(c for c, _ in outs) == n and all(c % n_chunk == 0 for c, _ in outs)
    return pl.pallas_call(
        functools.partial(_norm_proj_kernel, n_chunk=n_chunk),
        grid=(m // tm,),
        in_specs=[
            pl.BlockSpec((tm, d), lambda i: (i, 0)),
            pl.BlockSpec((1, d), lambda i: (0, 0)),
            pl.BlockSpec((d, n), lambda i: (0, 0)),
            pl.BlockSpec((1, n), lambda i: (0, 0)),
        ],
        out_specs=[pl.BlockSpec((tm, c), lambda i: (i, 0)) for c, _ in outs],
        out_shape=[jax.ShapeDtypeStruct((m, c), dt) for c, dt in outs],
        compiler_params=_params(("arbitrary",)),
        name="norm_proj",
    )(x2, g.reshape(1, d), w, colscale.reshape(1, n))


def _mix_ffn_kernel(*refs, n_parts, ff_chunk, final):
    x_ref = refs[0]
    part_refs = refs[1:1 + n_parts]
    wo_ref, g_ref, wg_ref, wu_ref, wd_ref = refs[1 + n_parts:6 + n_parts]
    fg_ref = refs[6 + n_parts] if final else None
    o_ref = refs[-1]

    y = jnp.concatenate([p_ref[...] for p_ref in part_refs], axis=1)
    x = x_ref[...] + _dot(y, wo_ref[...])
    h = _rms(x, g_ref[...], RMS_EPS).astype(BF16)
    d_ff = wg_ref.shape[1]
    acc = x
    for c in range(0, d_ff, ff_chunk):
        gate = _dot(h, wg_ref[:, c:c + ff_chunk])
        up = _dot(h, wu_ref[:, c:c + ff_chunk])
        a = (gate * jax.nn.sigmoid(gate) * up).astype(BF16)
        acc = acc + _dot(a, wd_ref[c:c + ff_chunk, :])
    if final:
        acc = _rms(acc, fg_ref[...], RMS_EPS)
    o_ref[...] = acc


def _mix_ffn(x2, parts, w_out, g, wg, wu, wd, final_g=None, *, tm=1024, ff_chunk=256):
    m, d = x2.shape
    d_ff = wg.shape[1]
    final = final_g is not None
    const = lambda i: (0, 0)
    in_specs = [pl.BlockSpec((tm, d), lambda i: (i, 0))]
    in_specs += [pl.BlockSpec((tm, p.shape[1]), lambda i: (i, 0)) for p in parts]
    in_specs += [
        pl.BlockSpec((d, d), const),
        pl.BlockSpec((1, d), const),
        pl.BlockSpec((d, d_ff), const),
        pl.BlockSpec((d, d_ff), const),
        pl.BlockSpec((d_ff, d), const),
    ]
    args = [x2, *parts, w_out, g.reshape(1, d), wg, wu, wd]
    if final:
        in_specs.append(pl.BlockSpec((1, d), const))
        args.append(final_g.reshape(1, d))
    return pl.pallas_call(
        functools.partial(_mix_ffn_kernel, n_parts=len(parts), ff_chunk=ff_chunk, final=final),
        grid=(m // tm,),
        in_specs=in_specs,
        out_specs=pl.BlockSpec((tm, d), lambda i: (i, 0)),
        out_shape=jax.ShapeDtypeStruct((m, d), F32),
        compiler_params=_params(("arbitrary",)),
        name="mix_ffn",
    )(*args)


def _branch_geometry(t, window, dilation, q_blk):
    seq = t // dilation
    half = window // (2 * dilation)
    if seq <= 2 * q_blk:
        q_blk = seq
    k_win = min(q_blk + 2 * half, seq)
    assert seq % q_blk == 0 and q_blk % HEAD_DIM == 0
    assert q_blk == 2 * half or seq == q_blk
    return seq, half, q_blk, k_win


def _dilated_kernel(slope_ref, q_ref, k_ref, v_ref, y_ref, so_ref, sl_ref, *bias_refs, q_blk, group):
    hp = pl.program_id(1)
    t = q_ref.shape[1]
    order = sorted(range(len(DILATED_PAIRS)), key=lambda i: DILATED_PAIRS[i][1])

    for step, bi in enumerate(order):
        window, d = DILATED_PAIRS[bi]
        seq, half, qb_rows, k_win = _branch_geometry(t, window, d, q_blk)
        bias_ref = bias_refs[bi]
        first, last = step == 0, step == len(order) - 1

        row = lax.broadcasted_iota(jnp.int32, (2 * qb_rows, k_win), 0)
        col = lax.broadcasted_iota(jnp.int32, (2 * qb_rows, k_win), 1)
        second = row >= qb_rows
        rel = col - jnp.where(second, row - qb_rows, row)
        slope2 = jnp.where(second, slope_ref[2 * hp + 1], slope_ref[2 * hp]) * (LOG2E * d)
        for si in range(bias_ref.shape[0]):
            dist = jnp.abs(rel - si * half)
            bias_ref[si] = jnp.where(dist <= half, -slope2 * dist.astype(F32), NEG_BIG)

        lane = lax.broadcasted_iota(jnp.int32, (qb_rows, LANES), 1)
        first_head = lane < HEAD_DIM
        n_blk = seq // qb_rows

        def tile_rows(idx, d=d, half=half, seq=seq, qb_rows=qb_rows, k_win=k_win, n_blk=n_blk):
            r = lax.div(idx, jnp.int32(n_blk))
            q0 = lax.rem(idx, jnp.int32(n_blk)) * qb_rows
            ws = jnp.clip(q0 - half, 0, seq - k_win)
            if d == 1:
                rows_q = pl.ds(pl.multiple_of(q0, qb_rows), qb_rows)
                rows_k = pl.ds(pl.multiple_of(ws, HEAD_DIM), k_win)
            else:
                rows_q = pl.ds(r + d * q0, qb_rows, stride=d)
                rows_k = pl.ds(r + d * ws, k_win, stride=d)
            return rows_q, rows_k, lax.div(q0 - ws, jnp.int32(half))

        def group_body(g, carry, group, d=d, qb_rows=qb_rows, bias_ref=bias_ref, first=first, last=last,
                       first_head=first_head, tile_rows=tile_rows):
            tiles = []
            for i in range(group):
                rows_q, rows_k, si = tile_rows(g * group + i)
                q = q_ref[0, rows_q, :].astype(BF16)
                zero = jnp.zeros_like(q)
                qq = jnp.concatenate([jnp.where(first_head, q, zero), jnp.where(first_head, zero, q)], axis=0)
                kw = k_ref[0, rows_k, :].astype(BF16)
                s = _dot_nt(qq, kw) + bias_ref[si]
                tiles.append((rows_q, rows_k, s))
            soft = []
            for rows_q, rows_k, s in tiles:
                mx = jnp.max(s, axis=-1, keepdims=True)
                p = jnp.exp2(s - mx)
                den = jnp.sum(p, axis=-1, keepdims=True)
                soft.append((rows_q, rows_k, p.astype(BF16), den, mx + jnp.log2(den)))
            for rows_q, rows_k, p, den, lse2 in soft:
                vw = v_ref[0, rows_k, :].astype(BF16)
                o2 = _dot(p, vw) / den
                o = jnp.where(first_head, o2[0:qb_rows], o2[qb_rows:2 * qb_rows])
                lse = jnp.where(first_head, lse2[0:qb_rows], lse2[qb_rows:2 * qb_rows])
                if not first:
                    o_prev = so_ref[rows_q, :]
                    l_prev = sl_ref[rows_q, :]
                    top = jnp.maximum(l_prev, lse)
                    w_prev = jnp.exp2(l_prev - top)
                    w_cur = jnp.exp2(lse - top)
                    tot = w_prev + w_cur
                    o = (o_prev * w_prev + o * w_cur) / tot
                    lse = top + jnp.log2(tot)
                so_ref[rows_q, :] = o
                if not last:
                    sl_ref[rows_q, :] = lse
            return carry

        n_total = d * n_blk
        grp = math.gcd(n_total, max(1, group * q_blk // qb_rows))
        lax.fori_loop(0, n_total // grp, functools.partial(group_body, group=grp), 0)

    for c in range(0, t, OUT_CHUNK):
        y_ref[0, c:c + OUT_CHUNK, :] = so_ref[c:c + OUT_CHUNK, :].astype(y_ref.dtype)


def _dilated_mixture(slopes, qkv_a, *, q_blk=128, group=8):
    b, t, _ = qkv_a.shape
    ab = W_A // LANES
    seq_blk = (1, t, LANES)
    bias_scratch = []
    for window, d in DILATED_PAIRS:
        seq, half, qb_rows, k_win = _branch_geometry(t, window, d, q_blk)
        n_shift = 1 if seq == qb_rows else 3
        bias_scratch.append(pltpu.VMEM((n_shift, 2 * qb_rows, k_win), F32))
    return pl.pallas_call(
        functools.partial(_dilated_kernel, q_blk=q_blk, group=group),
        grid=(b, ab),
        in_specs=[
            pl.BlockSpec(memory_space=pltpu.SMEM),
            pl.BlockSpec(seq_blk, lambda bi, hp: (bi, 0, hp)),
            pl.BlockSpec(seq_blk, lambda bi, hp: (bi, 0, ab + hp)),
            pl.BlockSpec(seq_blk, lambda bi, hp: (bi, 0, 2 * ab + hp)),
        ],
        out_specs=pl.BlockSpec(seq_blk, lambda bi, hp: (bi, 0, hp)),
        out_shape=jax.ShapeDtypeStruct((b, t, W_A), BF16),
        scratch_shapes=[pltpu.VMEM((t, LANES), F32), pltpu.VMEM((t, LANES), F32)] + bias_scratch,
        compiler_params=_params(("arbitrary",) * 2),
        name="dilated_mixture",
    )(slopes, qkv_a, qkv_a, qkv_a)


def _natten_kernel(q_ref, k_ref, v_ref, tbl_ref, y_ref, *, rows, group):
    n_keys = NA_ROWS * GRID_W
    lane = lax.broadcasted_iota(jnp.int32, (GRID_W, LANES), 1)
    first_head = lane < HEAD_DIM

    def group_body(g, carry):
        tiles = []
        for i in range(group):
            r = g * group + i
            rs = jnp.clip(r - NA_ROWS // 2, 0, rows - NA_ROWS)
            q0 = pl.multiple_of(r * GRID_W, GRID_W)
            k0 = pl.multiple_of(rs * GRID_W, GRID_W)
            q = q_ref[0, pl.ds(q0, GRID_W), :]
            zero = jnp.zeros_like(q)
            qq = jnp.concatenate([jnp.where(first_head, q, zero), jnp.where(first_head, zero, q)], axis=0)
            s = _dot_nt(qq, k_ref[0, pl.ds(k0, n_keys), :]) + tbl_ref[0, r - rs]
            tiles.append((q0, k0, s))
        soft = []
        for q0, k0, s in tiles:
            mx = jnp.max(s, axis=-1, keepdims=True)
            p = jnp.exp2(s - mx)
            soft.append((q0, k0, p.astype(BF16), jnp.sum(p, axis=-1, keepdims=True)))
        for q0, k0, p, den in soft:
            o2 = _dot(p, v_ref[0, pl.ds(k0, n_keys), :]) / den
            o = jnp.where(first_head, o2[0:GRID_W], o2[GRID_W:2 * GRID_W])
            y_ref[0, pl.ds(q0, GRID_W), :] = o.astype(y_ref.dtype)
        return carry

    lax.fori_loop(0, rows // group, group_body, 0)


def _natten_bias_table(rpb):
    kh = NA_ROWS
    n_h = rpb.shape[0]
    c = np.arange(GRID_W)[:, None]
    kc = np.arange(GRID_W)[None, :]
    wc = np.clip(c - NA_COLS // 2, 0, GRID_W - NA_COLS)
    col_valid = (kc >= wc) & (kc < wc + NA_COLS)
    col_idx = kc - c + NA_COLS - 1
    pick = (col_idx[None] == np.arange(2 * NA_COLS - 1)[:, None, None]) & col_valid[None]
    band = jnp.einsum("hrm,mck->hrck", rpb.astype(F32) * LOG2E, jnp.asarray(pick, F32),
                      precision=lax.Precision.HIGHEST)
    band = jnp.where(col_valid[None, None], band, NEG_BIG)
    tbl = jnp.stack([band[:, kh - 1 - dl:2 * kh - 1 - dl] for dl in range(kh)], axis=1)
    tbl = tbl.reshape(n_h // 2, 2, kh, kh, GRID_W, GRID_W)
    tbl = tbl.transpose(0, 2, 1, 4, 3, 5)
    return tbl.reshape(n_h // 2, kh, 2 * GRID_W, kh * GRID_W)


def _natten(qkv_b, rpb, *, group=8):
    b, t, _ = qkv_b.shape
    rows = t // GRID_W
    assert rows >= NA_ROWS and t % GRID_W == 0 and rows % group == 0
    tbl = _natten_bias_table(rpb)
    bb = W_B // LANES
    seq_blk = (1, t, LANES)
    return pl.pallas_call(
        functools.partial(_natten_kernel, rows=rows, group=group),
        grid=(b, bb),
        in_specs=[
            pl.BlockSpec(seq_blk, lambda bi, hp: (bi, 0, hp)),
            pl.BlockSpec(seq_blk, lambda bi, hp: (bi, 0, bb + hp)),
            pl.BlockSpec(seq_blk, lambda bi, hp: (bi, 0, 2 * bb + hp)),
            pl.BlockSpec((1, NA_ROWS, 2 * GRID_W, NA_ROWS * GRID_W), lambda bi, hp: (hp, 0, 0, 0)),
        ],
        out_specs=pl.BlockSpec(seq_blk, lambda bi, hp: (bi, 0, hp)),
        out_shape=jax.ShapeDtypeStruct((b, t, W_B), BF16),
        compiler_params=_params(("arbitrary",) * 2),
        name="natten",
    )(qkv_b, qkv_b, qkv_b, tbl)


def _diff_attn_kernel(slope_ref, q_ref, k_ref, v_ref, lq1_ref, lk1_ref, lq2_ref, lk2_ref,
                      subln_ref, y_ref, qq_ref, rel_ref, vt_ref, s_ref, mx_ref,
                      m_ref, acc_ref, *, tq, tk, n_sub, lam_init):
    h = pl.program_id(1)
    qi = pl.program_id(2)
    n_kv = k_ref.shape[1] // tk
    slope2 = slope_ref[h] * LOG2E

    @pl.when(qi == 0)
    def _per_head_setup():
        key = lax.broadcasted_iota(jnp.int32, (tk, 2 * tq), 0)
        qry = lax.broadcasted_iota(jnp.int32, (tk, 2 * tq), 1)
        qry = jnp.where(qry >= tq, qry - tq, qry)
        rel = (key - qry).astype(F32) * slope2
        rel_ref[0] = rel
        rel_ref[1] = -rel
        ones_row = lax.broadcasted_iota(jnp.int32, (DEN_ROWS, tk), 0) == 0
        for c in range(n_kv):
            vt_ref[c, 0:LANES, :] = v_ref[0, c * tk:(c + 1) * tk, :].astype(F32).T.astype(BF16)
            vt_ref[c, LANES:LANES + DEN_ROWS, :] = jnp.where(ones_row, 1.0, 0.0).astype(BF16)

    lam = (jnp.exp(jnp.sum(lq1_ref[...] * lk1_ref[...], axis=-1, keepdims=True))
           - jnp.exp(jnp.sum(lq2_ref[...] * lk2_ref[...], axis=-1, keepdims=True)) + lam_init)

    def tile_fns(u):
        i0 = (qi * n_sub + u) * tq
        diag = i0 // tk

        def block_offset(j):
            return (j * tk - i0).astype(F32) * slope2

        def produce(j, slot, overlapping):
            j0 = pl.multiple_of(j * tk, tk)
            s = _dot_nt(k_ref[0, pl.ds(j0, tk), :], qq_ref[u])
            if overlapping:
                s = s - jnp.abs(rel_ref[0] + block_offset(j))
            else:
                s = s + rel_ref[(j > diag).astype(jnp.int32)]
            s_ref[u, slot] = s
            mx_ref[u, slot] = jnp.max(s, axis=0, keepdims=True)

        def consume(j, slot, overlapping):
            if overlapping:
                shift = jnp.zeros((), F32)
            else:
                off = block_offset(j)
                shift = jnp.where(j > diag, -off, off)
            m_prev = m_ref[u]
            m_new = jnp.maximum(m_prev, mx_ref[u, slot] + shift)
            alpha = jnp.exp2(m_prev - m_new)
            p = jnp.exp2((s_ref[u, slot] - (m_new - shift)).astype(BF16))
            acc_ref[u] = alpha * acc_ref[u] + _dot(vt_ref[j], p)
            m_ref[u] = m_new

        def block_at(t):
            if t == 0:
                return diag
            return (t - 1) + ((t - 1) >= diag).astype(jnp.int32)

        def start():
            q = q_ref[0, u * tq:(u + 1) * tq, :]
            lane = lax.broadcasted_iota(jnp.int32, (tq, LANES), 1)
            zero = jnp.zeros_like(q)
            qq_ref[u, 0:tq, :] = jnp.where(lane < HEAD_DIM, q, zero)
            qq_ref[u, tq:2 * tq, :] = jnp.where(lane < HEAD_DIM, zero, q)
            m_ref[u] = jnp.full(m_ref.shape[1:], NEG_BIG, F32)
            acc_ref[u] = jnp.zeros(acc_ref.shape[1:], F32)
            produce(block_at(0), 0, True)

        def middle():
            for t in range(n_kv):
                if t + 1 < n_kv:
                    produce(block_at(t + 1), (t + 1) % 2, False)
                consume(block_at(t), t % 2, t == 0)

        def finish():
            o = acc_ref[u, 0:LANES, :] / acc_ref[u, LANES:LANES + 1, :]
            y = (o[:, 0:tq] - lam * o[:, tq:2 * tq]).T
            y = _rms(y, subln_ref[...], SUBLN_EPS) * (1.0 - lam_init)
            y_ref[0, u * tq:(u + 1) * tq, :] = y.astype(y_ref.dtype)

        return start, middle, finish

    fns = [tile_fns(u) for u in range(n_sub)]
    fns[0][0]()
    for u in range(n_sub):
        fns[u][1]()
        if u + 1 < n_sub:
            fns[u + 1][0]()
        fns[u][2]()


def _diff_attn(slopes, qkv, lq1, lk1, lq2, lk2, subln, lam_init, *, tq=256, tk=512, n_sub=4):
    b, t, n_cols = qkv.shape
    nh = n_cols // (3 * LANES)
    assert t % tk == 0 and tk % tq == 0 and t % (n_sub * tq) == 0
    tqs = n_sub * tq
    vec = lambda a: a.reshape(1, -1).astype(F32)
    small = lambda n: pl.BlockSpec((1, n), lambda bi, h, qi: (0, 0))
    return pl.pallas_call(
        functools.partial(_diff_attn_kernel, tq=tq, tk=tk, n_sub=n_sub, lam_init=lam_init),
        grid=(b, nh, t // tqs),
        in_specs=[
            pl.BlockSpec(memory_space=pltpu.SMEM),
            pl.BlockSpec((1, tqs, LANES), lambda bi, h, qi: (bi, qi, h)),
            pl.BlockSpec((1, t, LANES), lambda bi, h, qi: (bi, 0, nh + h)),
            pl.BlockSpec((1, t, LANES), lambda bi, h, qi: (bi, 0, 2 * nh + h)),
            small(HEAD_DIM), small(HEAD_DIM), small(HEAD_DIM), small(HEAD_DIM),
            small(2 * HEAD_DIM),
        ],
        out_specs=pl.BlockSpec((1, tqs, LANES), lambda bi, h, qi: (bi, qi, h)),
        out_shape=jax.ShapeDtypeStruct((b, t, nh * LANES), BF16),
        scratch_shapes=[
            pltpu.VMEM((n_sub, 2 * tq, LANES), BF16),
            pltpu.VMEM((2, tk, 2 * tq), F32),
            pltpu.VMEM((t // tk, LANES + DEN_ROWS, tk), BF16),
            pltpu.VMEM((n_sub, 2, tk, 2 * tq), F32),
            pltpu.VMEM((n_sub, 2, 1, 2 * tq), F32),
            pltpu.VMEM((n_sub, 1, 2 * tq), F32),
            pltpu.VMEM((n_sub, LANES + DEN_ROWS, 2 * tq), F32),
        ],
        compiler_params=_params(("arbitrary",) * 3),
        name="diff_attn",
    )(slopes, qkv, qkv, qkv, vec(lq1), vec(lk1), vec(lq2), vec(lk2), vec(subln))


def _alibi_slopes(n):
    return jnp.exp2(-8.0 * jnp.arange(1, n + 1, dtype=F32) / n)


def _lambda_init(layer):
    return 0.8 - 0.6 * math.exp(-0.3 * layer)


def kernel(x, a0_norm, a0_w_in, a0_w_out, a0_rpb, f0_norm, f0_w_gate, f0_w_up, f0_w_down,
           a1_norm, a1_w_qkv, a1_w_out, a1_lam_q1, a1_lam_k1, a1_lam_q2, a1_lam_k2, a1_subln,
           f1_norm, f1_w_gate, f1_w_up, f1_w_down, final_norm):
    b, t, d = x.shape
    x2 = x.reshape(b * t, d)
    qscale = HEAD_DIM ** -0.5 * LOG2E
    bf = lambda w: w.astype(BF16)

    cs0 = np.ones((3 * W_A + 3 * W_B,), np.float32)
    cs0[0:W_A] = qscale
    cs0[3 * W_A:3 * W_A + W_B] = qscale
    qkv_a, qkv_b = _norm_proj(x2, a0_norm, bf(a0_w_in), jnp.asarray(cs0),
                              ((3 * W_A, F32), (3 * W_B, BF16)))
    ya = _dilated_mixture(_alibi_slopes(N_HEADS_A), qkv_a.reshape(b, t, 3 * W_A)).reshape(b * t, W_A)
    yb = _natten(qkv_b.reshape(b, t, 3 * W_B), a0_rpb).reshape(b * t, W_B)
    x2 = _mix_ffn(x2, [ya, yb], bf(a0_w_out), f0_norm, bf(f0_w_gate), bf(f0_w_up), bf(f0_w_down))

    wc = a1_w_qkv.shape[1] // 3
    cs1 = np.ones((3 * wc,), np.float32)
    cs1[0:wc] = qscale
    (qkv1,) = _norm_proj(x2, a1_norm, bf(a1_w_qkv), jnp.asarray(cs1), ((3 * wc, BF16),))
    qkv1 = qkv1.reshape(b, t, 3 * wc)
    n_heads_c = wc // (2 * HEAD_DIM)
    y = _diff_attn(_alibi_slopes(n_heads_c), qkv1, a1_lam_q1, a1_lam_k1, a1_lam_q2, a1_lam_k2,
                   a1_subln, _lambda_init(1)).reshape(b * t, wc)
    out = _mix_ffn(x2, [y], bf(a1_w_out), f1_norm, bf(f1_w_gate), bf(f1_w_up), bf(f1_w_down),
                   final_norm)
    return out.reshape(b, t, d)
```

```python
import functools
import math

import numpy as np
import jax
import jax.numpy as jnp
from jax import lax
from jax.experimental import pallas as pl
from jax.experimental.pallas import tpu as pltpu

HEAD_DIM = 64
N_HEADS_A = 8
N_HEADS_B = 8
W_A = N_HEADS_A * HEAD_DIM
W_B = N_HEADS_B * HEAD_DIM
DILATED_PAIRS = ((128, 1), (512, 4), (2048, 16))
GRID_W = 64
NA_ROWS = 8
NA_COLS = 16
RMS_EPS = 1e-6
SUBLN_EPS = 1e-5
LOG2E = math.log2(math.e)

LANES = 128
OUT_CHUNK = 512
DEN_ROWS = 16
V7X_VMEM_LIMIT_BYTES = 56 * 1024 * 1024

NEG_BIG = -1e30

BF16 = jnp.bfloat16
F32 = jnp.float32


def _params(semantics):
    return pltpu.CompilerParams(dimension_semantics=semantics,
                                vmem_limit_bytes=V7X_VMEM_LIMIT_BYTES)


def _rms(x, g, eps):
    ms = jnp.mean(x * x, axis=-1, keepdims=True)
    return (x * lax.rsqrt(ms + eps)) * g


def _dot_nt(a, b):
    return lax.dot_general(a, b, (((1,), (1,)), ((), ())), preferred_element_type=F32)


def _dot(a, b):
    return jnp.dot(a, b, preferred_element_type=F32)


def _norm_proj_kernel(x_ref, g_ref, w_ref, cs_ref, *o_refs, n_chunk):
    h = _rms(x_ref[...], g_ref[...], RMS_EPS).astype(BF16)
    col = 0
    for o_ref in o_refs:
        for c in range(0, o_ref.shape[1], n_chunk):
            acc = _dot(h, w_ref[:, col + c:col + c + n_chunk])
            o_ref[:, c:c + n_chunk] = (acc * cs_ref[:, col + c:col + c + n_chunk]).astype(o_ref.dtype)
        col += o_ref.shape[1]


def _norm_proj(x2, g, w, colscale, outs, *, tm=1024, n_chunk=512):
    m, d = x2.shape
    n = w.shape[1]
    assert sum(c for c, _ in outs) == n and all(c % n_chunk == 0 for c, _ in outs)
    return pl.pallas_call(
        functools.partial(_norm_proj_kernel, n_chunk=n_chunk),
        grid=(m // tm,),
        in_specs=[
            pl.BlockSpec((tm, d), lambda i: (i, 0)),
            pl.BlockSpec((1, d), lambda i: (0, 0)),
            pl.BlockSpec((d, n), lambda i: (0, 0)),
            pl.BlockSpec((1, n), lambda i: (0, 0)),
        ],
        out_specs=[pl.BlockSpec((tm, c), lambda i: (i, 0)) for c, _ in outs],
        out_shape=[jax.ShapeDtypeStruct((m, c), dt) for c, dt in outs],
        compiler_params=_params(("arbitrary",)),
        name="norm_proj",
    )(x2, g.reshape(1, d), w, colscale.reshape(1, n))


def _mix_ffn_kernel(*refs, n_parts, ff_chunk, final):
    x_ref = refs[0]
    part_refs = refs[1:1 + n_parts]
    wo_ref, g_ref, wg_ref, wu_ref, wd_ref = refs[1 + n_parts:6 + n_parts]
    fg_ref = refs[6 + n_parts] if final else None
    o_ref = refs[-1]

    y = jnp.concatenate([p_ref[...] for p_ref in part_refs], axis=1)
    x = x_ref[...] + _dot(y, wo_ref[...])
    h = _rms(x, g_ref[...], RMS_EPS).astype(BF16)
    d_ff = wg_ref.shape[1]
    acc = x
    for c in range(0, d_ff, ff_chunk):
        gate = _dot(h, wg_ref[:, c:c + ff_chunk])
        up = _dot(h, wu_ref[:, c:c + ff_chunk])
        a = (gate * jax.nn.sigmoid(gate) * up).astype(BF16)
        acc = acc + _dot(a, wd_ref[c:c + ff_chunk, :])
    if final:
        acc = _rms(acc, fg_ref[...], RMS_EPS)
    o_ref[...] = acc


def _mix_ffn(x2, parts, w_out, g, wg, wu, wd, final_g=None, *, tm=1024, ff_chunk=256):
    m, d = x2.shape
    d_ff = wg.shape[1]
    final = final_g is not None
    const = lambda i: (0, 0)
    in_specs = [pl.BlockSpec((tm, d), lambda i: (i, 0))]
    in_specs += [pl.BlockSpec((tm, p.shape[1]), lambda i: (i, 0)) for p in parts]
    in_specs += [
        pl.BlockSpec((d, d), const),
        pl.BlockSpec((1, d), const),
        pl.BlockSpec((d, d_ff), const),
        pl.BlockSpec((d, d_ff), const),
        pl.BlockSpec((d_ff, d), const),
    ]
    args = [x2, *parts, w_out, g.reshape(1, d), wg, wu, wd]
    if final:
        in_specs.append(pl.BlockSpec((1, d), const))
        args.append(final_g.reshape(1, d))
    return pl.pallas_call(
        functools.partial(_mix_ffn_kernel, n_parts=len(parts), ff_chunk=ff_chunk, final=final),
        grid=(m // tm,),
        in_specs=in_specs,
        out_specs=pl.BlockSpec((tm, d), lambda i: (i, 0)),
        out_shape=jax.ShapeDtypeStruct((m, d), F32),
        compiler_params=_params(("arbitrary",)),
        name="mix_ffn",
    )(*args)


def _branch_geometry(t, window, dilation, q_blk):
    seq = t // dilation
    half = window // (2 * dilation)
    if seq <= 2 * q_blk:
        q_blk = seq
    k_win = min(q_blk + 2 * half, seq)
    assert seq % q_blk == 0 and q_blk % HEAD_DIM == 0
    assert q_blk == 2 * half or seq == q_blk
    return seq, half, q_blk, k_win


def _dilated_kernel(slope_ref, q_ref, k_ref, v_ref, y_ref, so_ref, sl_ref, *bias_refs, q_blk, group):
    hp = pl.program_id(1)
    t = q_ref.shape[1]
    order = sorted(range(len(DILATED_PAIRS)), key=lambda i: DILATED_PAIRS[i][1])

    for step, bi in enumerate(order):
        window, d = DILATED_PAIRS[bi]
        seq, half, qb_rows, k_win = _branch_geometry(t, window, d, q_blk)
        bias_ref = bias_refs[bi]
        first, last = step == 0, step == len(order) - 1

        row = lax.broadcasted_iota(jnp.int32, (2 * qb_rows, k_win), 0)
        col = lax.broadcasted_iota(jnp.int32, (2 * qb_rows, k_win), 1)
        second = row >= qb_rows
        rel = col - jnp.where(second, row - qb_rows, row)
        slope2 = jnp.where(second, slope_ref[2 * hp + 1], slope_ref[2 * hp]) * (LOG2E * d)
        for si in range(bias_ref.shape[0]):
            dist = jnp.abs(rel - si * half)
            bias_ref[si] = jnp.where(dist <= half, -slope2 * dist.astype(F32), NEG_BIG)

        lane = lax.broadcasted_iota(jnp.int32, (qb_rows, LANES), 1)
        first_head = lane < HEAD_DIM
        n_blk = seq // qb_rows

        def tile_rows(idx, d=d, half=half, seq=seq, qb_rows=qb_rows, k_win=k_win, n_blk=n_blk):
            r = lax.div(idx, jnp.int32(n_blk))
            q0 = lax.rem(idx, jnp.int32(n_blk)) * qb_rows
            ws = jnp.clip(q0 - half, 0, seq - k_win)
            if d == 1:
                rows_q = pl.ds(pl.multiple_of(q0, qb_rows), qb_rows)
                rows_k = pl.ds(pl.multiple_of(ws, HEAD_DIM), k_win)
            else:
                rows_q = pl.ds(r + d * q0, qb_rows, stride=d)
                rows_k = pl.ds(r + d * ws, k_win, stride=d)
            return rows_q, rows_k, lax.div(q0 - ws, jnp.int32(half))

        def group_body(g, carry, group, d=d, qb_rows=qb_rows, bias_ref=bias_ref, first=first, last=last,
                       first_head=first_head, tile_rows=tile_rows):
            tiles = []
            for i in range(group):
                rows_q, rows_k, si = tile_rows(g * group + i)
                q = q_ref[0, rows_q, :].astype(BF16)
                zero = jnp.zeros_like(q)
                qq = jnp.concatenate([jnp.where(first_head, q, zero), jnp.where(first_head, zero, q)], axis=0)
                kw = k_ref[0, rows_k, :].astype(BF16)
                s = _dot_nt(qq, kw) + bias_ref[si]
                tiles.append((rows_q, rows_k, s))
            soft = []
            for rows_q, rows_k, s in tiles:
                mx = jnp.max(s, axis=-1, keepdims=True)
                p = jnp.exp2(s - mx)
                den = jnp.sum(p, axis=-1, keepdims=True)
                soft.append((rows_q, rows_k, p.astype(BF16), den, mx + jnp.log2(den)))
            for rows_q, rows_k, p, den, lse2 in soft:
                vw = v_ref[0, rows_k, :].astype(BF16)
                o2 = _dot(p, vw) / den
                o = jnp.where(first_head, o2[0:qb_rows], o2[qb_rows:2 * qb_rows])
                lse = jnp.where(first_head, lse2[0:qb_rows], lse2[qb_rows:2 * qb_rows])
                if not first:
                    o_prev = so_ref[rows_q, :]
                    l_prev = sl_ref[rows_q, :]
                    top = jnp.maximum(l_prev, lse)
                    w_prev = jnp.exp2(l_prev - top)
                    w_cur = jnp.exp2(lse - top)
                    tot = w_prev + w_cur
                    o = (o_prev * w_prev + o * w_cur) / tot
                    lse = top + jnp.log2(tot)
                so_ref[rows_q, :] = o
                if not last:
                    sl_ref[rows_q, :] = lse
            return carry

        n_total = d * n_blk
        grp = math.gcd(n_total, max(1, group * q_blk // qb_rows))
        lax.fori_loop(0, n_total // grp, functools.partial(group_body, group=grp), 0)

    for c in range(0, t, OUT_CHUNK):
        y_ref[0, c:c + OUT_CHUNK, :] = so_ref[c:c + OUT_CHUNK, :].astype(y_ref.dtype)


def _dilated_mixture(slopes, qkv_a, *, q_blk=128, group=8):
    b, t, _ = qkv_a.shape
    ab = W_A // LANES
    seq_blk = (1, t, LANES)
    bias_scratch = []
    for window, d in DILATED_PAIRS:
        seq, half, qb_rows, k_win = _branch_geometry(t, window, d, q_blk)
        n_shift = 1 if seq == qb_rows else 3
        bias_scratch.append(pltpu.VMEM((n_shift, 2 * qb_rows, k_win), F32))
    return pl.pallas_call(
        functools.partial(_dilated_kernel, q_blk=q_blk, group=group),
        grid=(b, ab),
        in_specs=[
            pl.BlockSpec(memory_space=pltpu.SMEM),
            pl.BlockSpec(seq_blk, lambda bi, hp: (bi, 0, hp)),
            pl.BlockSpec(seq_blk, lambda bi, hp: (bi, 0, ab + hp)),
            pl.BlockSpec(seq_blk, lambda bi, hp: (bi, 0, 2 * ab + hp)),
        ],
        out_specs=pl.BlockSpec(seq_blk, lambda bi, hp: (bi, 0, hp)),
        out_shape=jax.ShapeDtypeStruct((b, t, W_A), BF16),
        scratch_shapes=[pltpu.VMEM((t, LANES), F32), pltpu.VMEM((t, LANES), F32)] + bias_scratch,
        compiler_params=_params(("arbitrary",) * 2),
        name="dilated_mixture",
    )(slopes, qkv_a, qkv_a, qkv_a)


def _natten_kernel(q_ref, k_ref, v_ref, tbl_ref, y_ref, *, rows, group):
    n_keys = NA_ROWS * GRID_W
    lane = lax.broadcasted_iota(jnp.int32, (GRID_W, LANES), 1)
    first_head = lane < HEAD_DIM

    def group_body(g, carry):
        tiles = []
        for i in range(group):
            r = g * group + i
            rs = jnp.clip(r - NA_ROWS // 2, 0, rows - NA_ROWS)
            q0 = pl.multiple_of(r * GRID_W, GRID_W)
            k0 = pl.multiple_of(rs * GRID_W, GRID_W)
            q = q_ref[0, pl.ds(q0, GRID_W), :]
            zero = jnp.zeros_like(q)
            qq = jnp.concatenate([jnp.where(first_head, q, zero), jnp.where(first_head, zero, q)], axis=0)
            s = _dot_nt(qq, k_ref[0, pl.ds(k0, n_keys), :]) + tbl_ref[0, r - rs]
            tiles.append((q0, k0, s))
        soft = []
        for q0, k0, s in tiles:
            mx = jnp.max(s, axis=-1, keepdims=True)
            p = jnp.exp2(s - mx)
            soft.append((q0, k0, p.astype(BF16), jnp.sum(p, axis=-1, keepdims=True)))
        for q0, k0, p, den in soft:
            o2 = _dot(p, v_ref[0, pl.ds(k0, n_keys), :]) / den
            o = jnp.where(first_head, o2[0:GRID_W], o2[GRID_W:2 * GRID_W])
            y_ref[0, pl.ds(q0, GRID_W), :] = o.astype(y_ref.dtype)
        return carry

    lax.fori_loop(0, rows // group, group_body, 0)


def _natten_bias_table(rpb):
    kh = NA_ROWS
    n_h = rpb.shape[0]
    c = np.arange(GRID_W)[:, None]
    kc = np.arange(GRID_W)[None, :]
    wc = np.clip(c - NA_COLS // 2, 0, GRID_W - NA_COLS)
    col_valid = (kc >= wc) & (kc < wc + NA_COLS)
    col_idx = kc - c + NA_COLS - 1
    pick = (col_idx[None] == np.arange(2 * NA_COLS - 1)[:, None, None]) & col_valid[None]
    band = jnp.einsum("hrm,mck->hrck", rpb.astype(F32) * LOG2E, jnp.asarray(pick, F32),
                      precision=lax.Precision.HIGHEST)
    band = jnp.where(col_valid[None, None], band, NEG_BIG)
    tbl = jnp.stack([band[:, kh - 1 - dl:2 * kh - 1 - dl] for dl in range(kh)], axis=1)
    tbl = tbl.reshape(n_h // 2, 2, kh, kh, GRID_W, GRID_W)
    tbl = tbl.transpose(0, 2, 1, 4, 3, 5)
    return tbl.reshape(n_h // 2, kh, 2 * GRID_W, kh * GRID_W)


def _natten(qkv_b, rpb, *, group=8):
    b, t, _ = qkv_b.shape
    rows = t // GRID_W
    assert rows >= NA_ROWS and t % GRID_W == 0 and rows % group == 0
    tbl = _natten_bias_table(rpb)
    bb = W_B // LANES
    seq_blk = (1, t, LANES)
    return pl.pallas_call(
        functools.partial(_natten_kernel, rows=rows, group=group),
        grid=(b, bb),
        in_specs=[
            pl.BlockSpec(seq_blk, lambda bi, hp: (bi, 0, hp)),
            pl.BlockSpec(seq_blk, lambda bi, hp: (bi, 0, bb + hp)),
            pl.BlockSpec(seq_blk, lambda bi, hp: (bi, 0, 2 * bb + hp)),
            pl.BlockSpec((1, NA_ROWS, 2 * GRID_W, NA_ROWS * GRID_W), lambda bi, hp: (hp, 0, 0, 0)),
        ],
        out_specs=pl.BlockSpec(seq_blk, lambda bi, hp: (bi, 0, hp)),
        out_shape=jax.ShapeDtypeStruct((b, t, W_B), BF16),
        compiler_params=_params(("arbitrary",) * 2),
        name="natten",
    )(qkv_b, qkv_b, qkv_b, tbl)


def _diff_attn_kernel(slope_ref, q_ref, k_ref, v_ref, lq1_ref, lk1_ref, lq2_ref, lk2_ref,
                      subln_ref, y_ref, qq_ref, rel_ref, vt_ref, s_ref, mx_ref,
                      m_ref, acc_ref, *, tq, tk, n_sub, lam_init):
    h = pl.program_id(1)
    qi = pl.program_id(2)
    n_kv = k_ref.shape[1] // tk
    slope2 = slope_ref[h] * LOG2E

    @pl.when(qi == 0)
    def _per_head_setup():
        key = lax.broadcasted_iota(jnp.int32, (tk, 2 * tq), 0)
        qry = lax.broadcasted_iota(jnp.int32, (tk, 2 * tq), 1)
        qry = jnp.where(qry >= tq, qry - tq, qry)
        rel = (key - qry).astype(F32) * slope2
        rel_ref[0] = rel
        rel_ref[1] = -rel
        ones_row = lax.broadcasted_iota(jnp.int32, (DEN_ROWS, tk), 0) == 0
        for c in range(n_kv):
            vt_ref[c, 0:LANES, :] = v_ref[0, c * tk:(c + 1) * tk, :].astype(F32).T.astype(BF16)
            vt_ref[c, LANES:LANES + DEN_ROWS, :] = jnp.where(ones_row, 1.0, 0.0).astype(BF16)

    lam = (jnp.exp(jnp.sum(lq1_ref[...] * lk1_ref[...], axis=-1, keepdims=True))
           - jnp.exp(jnp.sum(lq2_ref[...] * lk2_ref[...], axis=-1, keepdims=True)) + lam_init)

    def tile_fns(u):
        i0 = (qi * n_sub + u) * tq
        diag = i0 // tk

        def block_offset(j):
            return (j * tk - i0).astype(F32) * slope2

        def produce(j, slot, overlapping):
            j0 = pl.multiple_of(j * tk, tk)
            s = _dot_nt(k_ref[0, pl.ds(j0, tk), :], qq_ref[u])
            if overlapping:
                s = s - jnp.abs(rel_ref[0] + block_offset(j))
            else:
                s = s + rel_ref[(j > diag).astype(jnp.int32)]
            s_ref[u, slot] = s
            mx_ref[u, slot] = jnp.max(s, axis=0, keepdims=True)

        def consume(j, slot, overlapping):
            if overlapping:
                shift = jnp.zeros((), F32)
            else:
                off = block_offset(j)
                shift = jnp.where(j > diag, -off, off)
            m_prev = m_ref[u]
            m_new = jnp.maximum(m_prev, mx_ref[u, slot] + shift)
            alpha = jnp.exp2(m_prev - m_new)
            p = jnp.exp2((s_ref[u, slot] - (m_new - shift)).astype(BF16))
            acc_ref[u] = alpha * acc_ref[u] + _dot(vt_ref[j], p)
            m_ref[u] = m_new

        def block_at(t):
            if t == 0:
                return diag
            return (t - 1) + ((t - 1) >= diag).astype(jnp.int32)

        def start():
            q = q_ref[0, u * tq:(u + 1) * tq, :]
            lane = lax.broadcasted_iota(jnp.int32, (tq, LANES), 1)
            zero = jnp.zeros_like(q)
            qq_ref[u, 0:tq, :] = jnp.where(lane < HEAD_DIM, q, zero)
            qq_ref[u, tq:2 * tq, :] = jnp.where(lane < HEAD_DIM, zero, q)
            m_ref[u] = jnp.full(m_ref.shape[1:], NEG_BIG, F32)
            acc_ref[u] = jnp.zeros(acc_ref.shape[1:], F32)
            produce(block_at(0), 0, True)

        def middle():
            for t in range(n_kv):
                if t + 1 < n_kv:
                    produce(block_at(t + 1), (t + 1) % 2, False)
                consume(block_at(t), t % 2, t == 0)

        def finish():
            o = acc_ref[u, 0:LANES, :] / acc_ref[u, LANES:LANES + 1, :]
            y = (o[:, 0:tq] - lam * o[:, tq:2 * tq]).T
            y = _rms(y, subln_ref[...], SUBLN_EPS) * (1.0 - lam_init)
            y_ref[0, u * tq:(u + 1) * tq, :] = y.astype(y_ref.dtype)

        return start, middle, finish

    fns = [tile_fns(u) for u in range(n_sub)]
    fns[0][0]()
    for u in range(n_sub):
        fns[u][1]()
        if u + 1 < n_sub:
            fns[u + 1][0]()
        fns[u][2]()


def _diff_attn(slopes, qkv, lq1, lk1, lq2, lk2, subln, lam_init, *, tq=256, tk=512, n_sub=8):
    b, t, n_cols = qkv.shape
    nh = n_cols // (3 * LANES)
    assert t % tk == 0 and tk % tq == 0 and t % (n_sub * tq) == 0
    tqs = n_sub * tq
    vec = lambda a: a.reshape(1, -1).astype(F32)
    small = lambda n: pl.BlockSpec((1, n), lambda bi, h, qi: (0, 0))
    return pl.pallas_call(
        functools.partial(_diff_attn_kernel, tq=tq, tk=tk, n_sub=n_sub, lam_init=lam_init),
        grid=(b, nh, t // tqs),
        in_specs=[
            pl.BlockSpec(memory_space=pltpu.SMEM),
            pl.BlockSpec((1, tqs, LANES), lambda bi, h, qi: (bi, qi, h)),
            pl.BlockSpec((1, t, LANES), lambda bi, h, qi: (bi, 0, nh + h)),
            pl.BlockSpec((1, t, LANES), lambda bi, h, qi: (bi, 0, 2 * nh + h)),
            small(HEAD_DIM), small(HEAD_DIM), small(HEAD_DIM), small(HEAD_DIM),
            small(2 * HEAD_DIM),
        ],
        out_specs=pl.BlockSpec((1, tqs, LANES), lambda bi, h, qi: (bi, qi, h)),
        out_shape=jax.ShapeDtypeStruct((b, t, nh * LANES), BF16),
        scratch_shapes=[
            pltpu.VMEM((n_sub, 2 * tq, LANES), BF16),
            pltpu.VMEM((2, tk, 2 * tq), F32),
            pltpu.VMEM((t // tk, LANES + DEN_ROWS, tk), BF16),
            pltpu.VMEM((n_sub, 2, tk, 2 * tq), F32),
            pltpu.VMEM((n_sub, 2, 1, 2 * tq), F32),
            pltpu.VMEM((n_sub, 1, 2 * tq), F32),
            pltpu.VMEM((n_sub, LANES + DEN_ROWS, 2 * tq), F32),
        ],
        compiler_params=_params(("arbitrary",) * 3),
        name="diff_attn",
    )(slopes, qkv, qkv, qkv, vec(lq1), vec(lk1), vec(lq2), vec(lk2), vec(subln))


def _alibi_slopes(n):
    return jnp.exp2(-8.0 * jnp.arange(1, n + 1, dtype=F32) / n)


def _lambda_init(layer):
    return 0.8 - 0.6 * math.exp(-0.3 * layer)


def kernel(x, a0_norm, a0_w_in, a0_w_out, a0_rpb, f0_norm, f0_w_gate, f0_w_up, f0_w_down,
           a1_norm, a1_w_qkv, a1_w_out, a1_lam_q1, a1_lam_k1, a1_lam_q2, a1_lam_k2, a1_subln,
           f1_norm, f1_w_gate, f1_w_up, f1_w_down, final_norm):
    b, t, d = x.shape
    x2 = x.reshape(b * t, d)
    qscale = HEAD_DIM ** -0.5 * LOG2E
    bf = lambda w: w.astype(BF16)

    cs0 = np.ones((3 * W_A + 3 * W_B,), np.float32)
    cs0[0:W_A] = qscale
    cs0[3 * W_A:3 * W_A + W_B] = qscale
    qkv_a, qkv_b = _norm_proj(x2, a0_norm, bf(a0_w_in), jnp.asarray(cs0),
                              ((3 * W_A, F32), (3 * W_B, BF16)))
    ya = _dilated_mixture(_alibi_slopes(N_HEADS_A), qkv_a.reshape(b, t, 3 * W_A)).reshape(b * t, W_A)
    yb = _natten(qkv_b.reshape(b, t, 3 * W_B), a0_rpb).reshape(b * t, W_B)
    x2 = _mix_ffn(x2, [ya, yb], bf(a0_w_out), f0_norm, bf(f0_w_gate), bf(f0_w_up), bf(f0_w_down))

    wc = a1_w_qkv.shape[1] // 3
    cs1 = np.ones((3 * wc,), np.float32)
    cs1[0:wc] = qscale
    (qkv1,) = _norm_proj(x2, a1_norm, bf(a1_w_qkv), jnp.asarray(cs1), ((3 * wc, BF16),))
    qkv1 = qkv1.reshape(b, t, 3 * wc)
    n_heads_c = wc // (2 * HEAD_DIM)
    y = _diff_attn(_alibi_slopes(n_heads_c), qkv1, a1_lam_q1, a1_lam_k1, a1_lam_q2, a1_lam_k2,
                   a1_subln, _lambda_init(1)).reshape(b * t, wc)
    out = _mix_ffn(x2, [y], bf(a1_w_out), f1_norm, bf(f1_w_gate), bf(f1_w_up), bf(f1_w_down),
                   final_norm)
    return out.reshape(b, t, d)
```

```python
import functools
import math

import numpy as np
import jax
import jax.numpy as jnp
from jax import lax
from jax.experimental import pallas as pl
from jax.experimental.pallas import tpu as pltpu

HEAD_DIM = 64
N_HEADS_A = 8
N_HEADS_B = 8
W_A = N_HEADS_A * HEAD_DIM
W_B = N_HEADS_B * HEAD_DIM
DILATED_PAIRS = ((128, 1), (512, 4), (2048, 16))
GRID_W = 64
NA_ROWS = 8
NA_COLS = 16
RMS_EPS = 1e-6
SUBLN_EPS = 1e-5
LOG2E = math.log2(math.e)

LANES = 128
OUT_CHUNK = 512
DEN_ROWS = 16
V7X_VMEM_LIMIT_BYTES = 56 * 1024 * 1024

NEG_BIG = -1e30

BF16 = jnp.bfloat16
F32 = jnp.float32


def _params(semantics):
    return pltpu.CompilerParams(dimension_semantics=semantics,
                                vmem_limit_bytes=V7X_VMEM_LIMIT_BYTES)


def _rms(x, g, eps):
    ms = jnp.mean(x * x, axis=-1, keepdims=True)
    return (x * lax.rsqrt(ms + eps)) * g


def _dot_nt(a, b):
    return lax.dot_general(a, b, (((1,), (1,)), ((), ())), preferred_element_type=F32)


def _dot(a, b):
    return jnp.dot(a, b, preferred_element_type=F32)


def _norm_proj_kernel(x_ref, g_ref, w_ref, cs_ref, *o_refs, n_chunk):
    h = _rms(x_ref[...], g_ref[...], RMS_EPS).astype(BF16)
    col = 0
    for o_ref in o_refs:
        for c in range(0, o_ref.shape[1], n_chunk):
            acc = _dot(h, w_ref[:, col + c:col + c + n_chunk])
            o_ref[:, c:c + n_chunk] = (acc * cs_ref[:, col + c:col + c + n_chunk]).astype(o_ref.dtype)
        col += o_ref.shape[1]


def _norm_proj(x2, g, w, colscale, outs, *, tm=1024, n_chunk=512):
    m, d = x2.shape
    n = w.shape[1]
    assert sum(c for c, _ in outs) == n and all(c % n_chunk == 0 for c, _ in outs)
    return pl.pallas_call(
        functools.partial(_norm_proj_kernel, n_chunk=n_chunk),
        grid=(m // tm,),
        in_specs=[
            pl.BlockSpec((tm, d), lambda i: (i, 0)),
            pl.BlockSpec((1, d), lambda i: (0, 0)),
            pl.BlockSpec((d, n), lambda i: (0, 0)),
            pl.BlockSpec((1, n), lambda i: (0, 0)),
        ],
        out_specs=[pl.BlockSpec((tm, c), lambda i: (i, 0)) for c, _ in outs],
        out_shape=[jax.ShapeDtypeStruct((m, c), dt) for c, dt in outs],
        compiler_params=_params(("arbitrary",)),
        name="norm_proj",
    )(x2, g.reshape(1, d), w, colscale.reshape(1, n))


def _mix_ffn_kernel(*refs, n_parts, ff_chunk, final):
    x_ref = refs[0]
    part_refs = refs[1:1 + n_parts]
    wo_ref, g_ref, wg_ref, wu_ref, wd_ref = refs[1 + n_parts:6 + n_parts]
    fg_ref = refs[6 + n_parts] if final else None
    o_ref = refs[-1]

    y = jnp.concatenate([p_ref[...] for p_ref in part_refs], axis=1)
    x = x_ref[...] + _dot(y, wo_ref[...])
    h = _rms(x, g_ref[...], RMS_EPS).astype(BF16)
    d_ff = wg_ref.shape[1]
    acc = x
    for c in range(0, d_ff, ff_chunk):
        gate = _dot(h, wg_ref[:, c:c + ff_chunk])
        up = _dot(h, wu_ref[:, c:c + ff_chunk])
        a = (gate * jax.nn.sigmoid(gate) * up).astype(BF16)
        acc = acc + _dot(a, wd_ref[c:c + ff_chunk, :])
    if final:
        acc = _rms(acc, fg_ref[...], RMS_EPS)
    o_ref[...] = acc


def _mix_ffn(x2, parts, w_out, g, wg, wu, wd, final_g=None, *, tm=1024, ff_chunk=256):
    m, d = x2.shape
    d_ff = wg.shape[1]
    final = final_g is not None
    const = lambda i: (0, 0)
    in_specs = [pl.BlockSpec((tm, d), lambda i: (i, 0))]
    in_specs += [pl.BlockSpec((tm, p.shape[1]), lambda i: (i, 0)) for p in parts]
    in_specs += [
        pl.BlockSpec((d, d), const),
        pl.BlockSpec((1, d), const),
        pl.BlockSpec((d, d_ff), const),
        pl.BlockSpec((d, d_ff), const),
        pl.BlockSpec((d_ff, d), const),
    ]
    args = [x2, *parts, w_out, g.reshape(1, d), wg, wu, wd]
    if final:
        in_specs.append(pl.BlockSpec((1, d), const))
        args.append(final_g.reshape(1, d))
    return pl.pallas_call(
        functools.partial(_mix_ffn_kernel, n_parts=len(parts), ff_chunk=ff_chunk, final=final),
        grid=(m // tm,),
        in_specs=in_specs,
        out_specs=pl.BlockSpec((tm, d), lambda i: (i, 0)),
        out_shape=jax.ShapeDtypeStruct((m, d), F32),
        compiler_params=_params(("arbitrary",)),
        name="mix_ffn",
    )(*args)


def _branch_geometry(t, window, dilation, q_blk):
    seq = t // dilation
    half = window // (2 * dilation)
    if seq <= 2 * q_blk:
        q_blk = seq
    k_win = min(q_blk + 2 * half, seq)
    assert seq % q_blk == 0 and q_blk % HEAD_DIM == 0
    assert q_blk == 2 * half or seq == q_blk
    return seq, half, q_blk, k_win


def _dilated_kernel(slope_ref, q_ref, k_ref, v_ref, y_ref, so_ref, sl_ref, *bias_refs, q_blk, group):
    hp = pl.program_id(1)
    t = q_ref.shape[1]
    order = sorted(range(len(DILATED_PAIRS)), key=lambda i: DILATED_PAIRS[i][1])

    for step, bi in enumerate(order):
        window, d = DILATED_PAIRS[bi]
        seq, half, qb_rows, k_win = _branch_geometry(t, window, d, q_blk)
        bias_ref = bias_refs[bi]
        first, last = step == 0, step == len(order) - 1

        row = lax.broadcasted_iota(jnp.int32, (2 * qb_rows, k_win), 0)
        col = lax.broadcasted_iota(jnp.int32, (2 * qb_rows, k_win), 1)
        second = row >= qb_rows
        rel = col - jnp.where(second, row - qb_rows, row)
        slope2 = jnp.where(second, slope_ref[2 * hp + 1], slope_ref[2 * hp]) * (LOG2E * d)
        for si in range(bias_ref.shape[0]):
            dist = jnp.abs(rel - si * half)
            bias_ref[si] = jnp.where(dist <= half, -slope2 * dist.astype(F32), NEG_BIG)

        lane = lax.broadcasted_iota(jnp.int32, (qb_rows, LANES), 1)
        first_head = lane < HEAD_DIM
        n_blk = seq // qb_rows

        def tile_rows(idx, d=d, half=half, seq=seq, qb_rows=qb_rows, k_win=k_win, n_blk=n_blk):
            r = lax.div(idx, jnp.int32(n_blk))
            q0 = lax.rem(idx, jnp.int32(n_blk)) * qb_rows
            ws = jnp.clip(q0 - half, 0, seq - k_win)
            if d == 1:
                rows_q = pl.ds(pl.multiple_of(q0, qb_rows), qb_rows)
                rows_k = pl.ds(pl.multiple_of(ws, HEAD_DIM), k_win)
            else:
                rows_q = pl.ds(r + d * q0, qb_rows, stride=d)
                rows_k = pl.ds(r + d * ws, k_win, stride=d)
            return rows_q, rows_k, lax.div(q0 - ws, jnp.int32(half))

        def group_body(g, carry, group, d=d, qb_rows=qb_rows, bias_ref=bias_ref, first=first, last=last,
                       first_head=first_head, tile_rows=tile_rows):
            tiles = []
            for i in range(group):
                rows_q, rows_k, si = tile_rows(g * group + i)
                q = q_ref[0, rows_q, :].astype(BF16)
                zero = jnp.zeros_like(q)
                qq = jnp.concatenate([jnp.where(first_head, q, zero), jnp.where(first_head, zero, q)], axis=0)
                kw = k_ref[0, rows_k, :].astype(BF16)
                s = _dot_nt(qq, kw) + bias_ref[si]
                tiles.append((rows_q, rows_k, s))
            soft = []
            for rows_q, rows_k, s in tiles:
                mx = jnp.max(s, axis=-1, keepdims=True)
                p = jnp.exp2(s - mx)
                den = jnp.sum(p, axis=-1, keepdims=True)
                soft.append((rows_q, rows_k, p.astype(BF16), den, mx + jnp.log2(den)))
            for rows_q, rows_k, p, den, lse2 in soft:
                vw = v_ref[0, rows_k, :].astype(BF16)
                o2 = _dot(p, vw) / den
                o = jnp.where(first_head, o2[0:qb_rows], o2[qb_rows:2 * qb_rows])
                lse = jnp.where(first_head, lse2[0:qb_rows], lse2[qb_rows:2 * qb_rows])
                if not first:
                    o_prev = so_ref[rows_q, :]
                    l_prev = sl_ref[rows_q, :]
                    top = jnp.maximum(l_prev, lse)
                    w_prev = jnp.exp2(l_prev - top)
                    w_cur = jnp.exp2(lse - top)
                    tot = w_prev + w_cur
                    o = (o_prev * w_prev + o * w_cur) / tot
                    lse = top + jnp.log2(tot)
                so_ref[rows_q, :] = o
                if not last:
                    sl_ref[rows_q, :] = lse
            return carry

        n_total = d * n_blk
        grp = math.gcd(n_total, max(1, group * q_blk // qb_rows))
        lax.fori_loop(0, n_total // grp, functools.partial(group_body, group=grp), 0)

    for c in range(0, t, OUT_CHUNK):
        y_ref[0, c:c + OUT_CHUNK, :] = so_ref[c:c + OUT_CHUNK, :].astype(y_ref.dtype)


def _dilated_mixture(slopes, qkv_a, *, q_blk=128, group=8):
    b, t, _ = qkv_a.shape
    ab = W_A // LANES
    seq_blk = (1, t, LANES)
    bias_scratch = []
    for window, d in DILATED_PAIRS:
        seq, half, qb_rows, k_win = _branch_geometry(t, window, d, q_blk)
        n_shift = 1 if seq == qb_rows else 3
        bias_scratch.append(pltpu.VMEM((n_shift, 2 * qb_rows, k_win), F32))
    return pl.pallas_call(
        functools.partial(_dilated_kernel, q_blk=q_blk, group=group),
        grid=(b, ab),
        in_specs=[
            pl.BlockSpec(memory_space=pltpu.SMEM),
            pl.BlockSpec(seq_blk, lambda bi, hp: (bi, 0, hp)),
            pl.BlockSpec(seq_blk, lambda bi, hp: (bi, 0, ab + hp)),
            pl.BlockSpec(seq_blk, lambda bi, hp: (bi, 0, 2 * ab + hp)),
        ],
        out_specs=pl.BlockSpec(seq_blk, lambda bi, hp: (bi, 0, hp)),
        out_shape=jax.ShapeDtypeStruct((b, t, W_A), BF16),
        scratch_shapes=[pltpu.VMEM((t, LANES), F32), pltpu.VMEM((t, LANES), F32)] + bias_scratch,
        compiler_params=_params(("arbitrary",) * 2),
        name="dilated_mixture",
    )(slopes, qkv_a, qkv_a, qkv_a)


def _natten_kernel(q_ref, k_ref, v_ref, tbl_ref, y_ref, *, rows, group):
    n_keys = NA_ROWS * GRID_W
    lane = lax.broadcasted_iota(jnp.int32, (GRID_W, LANES), 1)
    first_head = lane < HEAD_DIM

    def group_body(g, carry):
        tiles = []
        for i in range(group):
            r = g * group + i
            rs = jnp.clip(r - NA_ROWS // 2, 0, rows - NA_ROWS)
            q0 = pl.multiple_of(r * GRID_W, GRID_W)
            k0 = pl.multiple_of(rs * GRID_W, GRID_W)
            q = q_ref[0, pl.ds(q0, GRID_W), :]
            zero = jnp.zeros_like(q)
            qq = jnp.concatenate([jnp.where(first_head, q, zero), jnp.where(first_head, zero, q)], axis=0)
            s = _dot_nt(qq, k_ref[0, pl.ds(k0, n_keys), :]) + tbl_ref[0, r - rs]
            tiles.append((q0, k0, s))
        soft = []
        for q0, k0, s in tiles:
            mx = jnp.max(s, axis=-1, keepdims=True)
            p = jnp.exp2(s - mx)
            soft.append((q0, k0, p.astype(BF16), jnp.sum(p, axis=-1, keepdims=True)))
        for q0, k0, p, den in soft:
            o2 = _dot(p, v_ref[0, pl.ds(k0, n_keys), :]) / den
            o = jnp.where(first_head, o2[0:GRID_W], o2[GRID_W:2 * GRID_W])
            y_ref[0, pl.ds(q0, GRID_W), :] = o.astype(y_ref.dtype)
        return carry

    lax.fori_loop(0, rows // group, group_body, 0)


def _natten_bias_table(rpb):
    kh = NA_ROWS
    n_h = rpb.shape[0]
    c = np.arange(GRID_W)[:, None]
    kc = np.arange(GRID_W)[None, :]
    wc = np.clip(c - NA_COLS // 2, 0, GRID_W - NA_COLS)
    col_valid = (kc >= wc) & (kc < wc + NA_COLS)
    col_idx = kc - c + NA_COLS - 1
    pick = (col_idx[None] == np.arange(2 * NA_COLS - 1)[:, None, None]) & col_valid[None]
    band = jnp.einsum("hrm,mck->hrck", rpb.astype(F32) * LOG2E, jnp.asarray(pick, F32),
                      precision=lax.Precision.HIGHEST)
    band = jnp.where(col_valid[None, None], band, NEG_BIG)
    tbl = jnp.stack([band[:, kh - 1 - dl:2 * kh - 1 - dl] for dl in range(kh)], axis=1)
    tbl = tbl.reshape(n_h // 2, 2, kh, kh, GRID_W, GRID_W)
    tbl = tbl.transpose(0, 2, 1, 4, 3, 5)
    return tbl.reshape(n_h // 2, kh, 2 * GRID_W, kh * GRID_W)


def _natten(qkv_b, rpb, *, group=8):
    b, t, _ = qkv_b.shape
    rows = t // GRID_W
    assert rows >= NA_ROWS and t % GRID_W == 0 and rows % group == 0
    tbl = _natten_bias_table(rpb)
    bb = W_B // LANES
    seq_blk = (1, t, LANES)
    return pl.pallas_call(
        functools.partial(_natten_kernel, rows=rows, group=group),
        grid=(b, bb),
        in_specs=[
            pl.BlockSpec(seq_blk, lambda bi, hp: (bi, 0, hp)),
            pl.BlockSpec(seq_blk, lambda bi, hp: (bi, 0, bb + hp)),
            pl.BlockSpec(seq_blk, lambda bi, hp: (bi, 0, 2 * bb + hp)),
            pl.BlockSpec((1, NA_ROWS, 2 * GRID_W, NA_ROWS * GRID_W), lambda bi, hp: (hp, 0, 0, 0)),
        ],
        out_specs=pl.BlockSpec(seq_blk, lambda bi, hp: (bi, 0, hp)),
        out_shape=jax.ShapeDtypeStruct((b, t, W_B), BF16),
        compiler_params=_params(("arbitrary",) * 2),
        name="natten",
    )(qkv_b, qkv_b, qkv_b, tbl)


def _diff_attn_kernel(slope_ref, q_ref, k_ref, v_ref, lq1_ref, lk1_ref, lq2_ref, lk2_ref,
                      subln_ref, y_ref, qq_ref, rel_ref, vt_ref, s_ref, mx_ref,
                      m_ref, acc_ref, *, tq, tk, n_sub, lam_init):
    h = pl.program_id(1)
    seq = k_ref.shape[1]
    n_kv = seq // tk
    slope2 = slope_ref[h] * LOG2E

    key = lax.broadcasted_iota(jnp.int32, (tk, 2 * tq), 0)
    qry = lax.broadcasted_iota(jnp.int32, (tk, 2 * tq), 1)
    qry = jnp.where(qry >= tq, qry - tq, qry)
    rel = (key - qry).astype(F32) * slope2
    rel_ref[0] = rel
    rel_ref[1] = -rel
    ones_row = lax.broadcasted_iota(jnp.int32, (DEN_ROWS, tk), 0) == 0
    for c in range(n_kv):
        vt_ref[c, 0:LANES, :] = v_ref[0, c * tk:(c + 1) * tk, :].astype(F32).T.astype(BF16)
        vt_ref[c, LANES:LANES + DEN_ROWS, :] = jnp.where(ones_row, 1.0, 0.0).astype(BF16)

    lam = (jnp.exp(jnp.sum(lq1_ref[...] * lk1_ref[...], axis=-1, keepdims=True))
           - jnp.exp(jnp.sum(lq2_ref[...] * lk2_ref[...], axis=-1, keepdims=True)) + lam_init)

    def tile_fns(u, first_tile):
        i0 = pl.multiple_of((first_tile + u) * tq, tq)
        diag = i0 // tk

        def block_offset(j):
            return (j * tk - i0).astype(F32) * slope2

        def produce(j, slot, overlapping):
            j0 = pl.multiple_of(j * tk, tk)
            s = _dot_nt(k_ref[0, pl.ds(j0, tk), :], qq_ref[u])
            if overlapping:
                s = s - jnp.abs(rel_ref[0] + block_offset(j))
            else:
                s = s + rel_ref[(j > diag).astype(jnp.int32)]
            s_ref[u, slot] = s
            mx_ref[u, slot] = jnp.max(s, axis=0, keepdims=True)

        def consume(j, slot, overlapping):
            if overlapping:
                shift = jnp.zeros((), F32)
            else:
                off = block_offset(j)
                shift = jnp.where(j > diag, -off, off)
            m_prev = m_ref[u]
            m_new = jnp.maximum(m_prev, mx_ref[u, slot] + shift)
            alpha = jnp.exp2(m_prev - m_new)
            p = jnp.exp2((s_ref[u, slot] - (m_new - shift)).astype(BF16))
            acc_ref[u] = alpha * acc_ref[u] + _dot(vt_ref[j], p)
            m_ref[u] = m_new

        def block_at(t):
            if t == 0:
                return diag
            return (t - 1) + ((t - 1) >= diag).astype(jnp.int32)

        def start():
            q = q_ref[0, pl.ds(i0, tq), :]
            lane = lax.broadcasted_iota(jnp.int32, (tq, LANES), 1)
            zero = jnp.zeros_like(q)
            qq_ref[u, 0:tq, :] = jnp.where(lane < HEAD_DIM, q, zero)
            qq_ref[u, tq:2 * tq, :] = jnp.where(lane < HEAD_DIM, zero, q)
            m_ref[u] = jnp.full(m_ref.shape[1:], NEG_BIG, F32)
            acc_ref[u] = jnp.zeros(acc_ref.shape[1:], F32)
            produce(block_at(0), 0, True)

        def middle():
            for t in range(n_kv):
                if t + 1 < n_kv:
                    produce(block_at(t + 1), (t + 1) % 2, False)
                consume(block_at(t), t % 2, t == 0)

        def finish():
            o = acc_ref[u, 0:LANES, :] / acc_ref[u, LANES:LANES + 1, :]
            y = (o[:, 0:tq] - lam * o[:, tq:2 * tq]).T
            y = _rms(y, subln_ref[...], SUBLN_EPS) * (1.0 - lam_init)
            y_ref[0, pl.ds(i0, tq), :] = y.astype(y_ref.dtype)

        return start, middle, finish

    def tile_group(g, carry):
        fns = [tile_fns(u, g * n_sub) for u in range(n_sub)]
        fns[0][0]()
        for u in range(n_sub):
            fns[u][1]()
            if u + 1 < n_sub:
                fns[u + 1][0]()
            fns[u][2]()
        return carry

    lax.fori_loop(0, seq // (n_sub * tq), tile_group, 0)


def _diff_attn(slopes, qkv, lq1, lk1, lq2, lk2, subln, lam_init, *, tq=256, tk=512, n_sub=8):
    b, t, n_cols = qkv.shape
    nh = n_cols // (3 * LANES)
    n_sub = math.gcd(n_sub, t // tq)
    assert t % tk == 0 and tk % tq == 0
    vec = lambda a: a.reshape(1, -1).astype(F32)
    small = lambda n: pl.BlockSpec((1, n), lambda bi, h: (0, 0))
    return pl.pallas_call(
        functools.partial(_diff_attn_kernel, tq=tq, tk=tk, n_sub=n_sub, lam_init=lam_init),
        grid=(b, nh),
        in_specs=[
            pl.BlockSpec(memory_space=pltpu.SMEM),
            pl.BlockSpec((1, t, LANES), lambda bi, h: (bi, 0, h)),
            pl.BlockSpec((1, t, LANES), lambda bi, h: (bi, 0, nh + h)),
            pl.BlockSpec((1, t, LANES), lambda bi, h: (bi, 0, 2 * nh + h)),
            small(HEAD_DIM), small(HEAD_DIM), small(HEAD_DIM), small(HEAD_DIM),
            small(2 * HEAD_DIM),
        ],
        out_specs=pl.BlockSpec((1, t, LANES), lambda bi, h: (bi, 0, h)),
        out_shape=jax.ShapeDtypeStruct((b, t, nh * LANES), BF16),
        scratch_shapes=[
            pltpu.VMEM((n_sub, 2 * tq, LANES), BF16),
            pltpu.VMEM((2, tk, 2 * tq), F32),
            pltpu.VMEM((t // tk, LANES + DEN_ROWS, tk), BF16),
            pltpu.VMEM((n_sub, 2, tk, 2 * tq), F32),
            pltpu.VMEM((n_sub, 2, 1, 2 * tq), F32),
            pltpu.VMEM((n_sub, 1, 2 * tq), F32),
            pltpu.VMEM((n_sub, LANES + DEN_ROWS, 2 * tq), F32),
        ],
        compiler_params=_params(("arbitrary",) * 2),
        name="diff_attn",
    )(slopes, qkv, qkv, qkv, vec(lq1), vec(lk1), vec(lq2), vec(lk2), vec(subln))


def _alibi_slopes(n):
    return jnp.exp2(-8.0 * jnp.arange(1, n + 1, dtype=F32) / n)


def _lambda_init(layer):
    return 0.8 - 0.6 * math.exp(-0.3 * layer)


def kernel(x, a0_norm, a0_w_in, a0_w_out, a0_rpb, f0_norm, f0_w_gate, f0_w_up, f0_w_down,
           a1_norm, a1_w_qkv, a1_w_out, a1_lam_q1, a1_lam_k1, a1_lam_q2, a1_lam_k2, a1_subln,
           f1_norm, f1_w_gate, f1_w_up, f1_w_down, final_norm):
    b, t, d = x.shape
    x2 = x.reshape(b * t, d)
    qscale = HEAD_DIM ** -0.5 * LOG2E
    bf = lambda w: w.astype(BF16)

    cs0 = np.ones((3 * W_A + 3 * W_B,), np.float32)
    cs0[0:W_A] = qscale
    cs0[3 * W_A:3 * W_A + W_B] = qscale
    qkv_a, qkv_b = _norm_proj(x2, a0_norm, bf(a0_w_in), jnp.asarray(cs0),
                              ((3 * W_A, F32), (3 * W_B, BF16)))
    ya = _dilated_mixture(_alibi_slopes(N_HEADS_A), qkv_a.reshape(b, t, 3 * W_A)).reshape(b * t, W_A)
    yb = _natten(qkv_b.reshape(b, t, 3 * W_B), a0_rpb).reshape(b * t, W_B)
    x2 = _mix_ffn(x2, [ya, yb], bf(a0_w_out), f0_norm, bf(f0_w_gate), bf(f0_w_up), bf(f0_w_down))

    wc = a1_w_qkv.shape[1] // 3
    cs1 = np.ones((3 * wc,), np.float32)
    cs1[0:wc] = qscale
    (qkv1,) = _norm_proj(x2, a1_norm, bf(a1_w_qkv), jnp.asarray(cs1), ((3 * wc, BF16),))
    qkv1 = qkv1.reshape(b, t, 3 * wc)
    n_heads_c = wc // (2 * HEAD_DIM)
    y = _diff_attn(_alibi_slopes(n_heads_c), qkv1, a1_lam_q1, a1_lam_k1, a1_lam_q2, a1_lam_k2,
                   a1_subln, _lambda_init(1)).reshape(b * t, wc)
    out = _mix_ffn(x2, [y], bf(a1_w_out), f1_norm, bf(f1_w_gate), bf(f1_w_up), bf(f1_w_down),
                   final_norm)
    return out.reshape(b, t, d)
```

```python
import functools
import math

import numpy as np
import jax
import jax.numpy as jnp
from jax import lax
from jax.experimental import pallas as pl
from jax.experimental.pallas import tpu as pltpu

HEAD_DIM = 64
N_HEADS_A = 8
N_HEADS_B = 8
W_A = N_HEADS_A * HEAD_DIM
W_B = N_HEADS_B * HEAD_DIM
DILATED_PAIRS = ((128, 1), (512, 4), (2048, 16))
GRID_W = 64
NA_ROWS = 8
NA_COLS = 16
RMS_EPS = 1e-6
SUBLN_EPS = 1e-5
LOG2E = math.log2(math.e)

LANES = 128
OUT_CHUNK = 512
DEN_ROWS = 16
V7X_VMEM_LIMIT_BYTES = 56 * 1024 * 1024

NEG_BIG = -1e30

BF16 = jnp.bfloat16
F32 = jnp.float32


def _params(semantics):
    return pltpu.CompilerParams(dimension_semantics=semantics,
                                vmem_limit_bytes=V7X_VMEM_LIMIT_BYTES)


def _rms(x, g, eps):
    ms = jnp.mean(x * x, axis=-1, keepdims=True)
    return (x * lax.rsqrt(ms + eps)) * g


def _dot_nt(a, b):
    return lax.dot_general(a, b, (((1,), (1,)), ((), ())), preferred_element_type=F32)


def _dot(a, b):
    return jnp.dot(a, b, preferred_element_type=F32)


def _norm_proj_kernel(x_ref, g_ref, w_ref, cs_ref, *o_refs, n_chunk):
    h = _rms(x_ref[...], g_ref[...], RMS_EPS).astype(BF16)
    col = 0
    for o_ref in o_refs:
        for c in range(0, o_ref.shape[1], n_chunk):
            acc = _dot(h, w_ref[:, col + c:col + c + n_chunk])
            o_ref[:, c:c + n_chunk] = (acc * cs_ref[:, col + c:col + c + n_chunk]).astype(o_ref.dtype)
        col += o_ref.shape[1]


def _norm_proj(x2, g, w, colscale, outs, *, tm=1024, n_chunk=512):
    m, d = x2.shape
    n = w.shape[1]
    assert sum(c for c, _ in outs) == n and all(c % n_chunk == 0 for c, _ in outs)
    return pl.pallas_call(
        functools.partial(_norm_proj_kernel, n_chunk=n_chunk),
        grid=(m // tm,),
        in_specs=[
            pl.BlockSpec((tm, d), lambda i: (i, 0)),
            pl.BlockSpec((1, d), lambda i: (0, 0)),
            pl.BlockSpec((d, n), lambda i: (0, 0)),
            pl.BlockSpec((1, n), lambda i: (0, 0)),
        ],
        out_specs=[pl.BlockSpec((tm, c), lambda i: (i, 0)) for c, _ in outs],
        out_shape=[jax.ShapeDtypeStruct((m, c), dt) for c, dt in outs],
        compiler_params=_params(("arbitrary",)),
        name="norm_proj",
    )(x2, g.reshape(1, d), w, colscale.reshape(1, n))


def _mix_ffn_kernel(*refs, n_parts, ff_chunk, final):
    x_ref = refs[0]
    part_refs = refs[1:1 + n_parts]
    wo_ref, g_ref, wg_ref, wu_ref, wd_ref = refs[1 + n_parts:6 + n_parts]
    fg_ref = refs[6 + n_parts] if final else None
    o_ref = refs[-1]

    y = jnp.concatenate([p_ref[...] for p_ref in part_refs], axis=1)
    x = x_ref[...] + _dot(y, wo_ref[...])
    h = _rms(x, g_ref[...], RMS_EPS).astype(BF16)
    d_ff = wg_ref.shape[1]
    acc = x
    for c in range(0, d_ff, ff_chunk):
        gate = _dot(h, wg_ref[:, c:c + ff_chunk])
        up = _dot(h, wu_ref[:, c:c + ff_chunk])
        a = (gate * jax.nn.sigmoid(gate) * up).astype(BF16)
        acc = acc + _dot(a, wd_ref[c:c + ff_chunk, :])
    if final:
        acc = _rms(acc, fg_ref[...], RMS_EPS)
    o_ref[...] = acc


def _mix_ffn(x2, parts, w_out, g, wg, wu, wd, final_g=None, *, tm=1024, ff_chunk=256):
    m, d = x2.shape
    d_ff = wg.shape[1]
    final = final_g is not None
    const = lambda i: (0, 0)
    in_specs = [pl.BlockSpec((tm, d), lambda i: (i, 0))]
    in_specs += [pl.BlockSpec((tm, p.shape[1]), lambda i: (i, 0)) for p in parts]
    in_specs += [
        pl.BlockSpec((d, d), const),
        pl.BlockSpec((1, d), const),
        pl.BlockSpec((d, d_ff), const),
        pl.BlockSpec((d, d_ff), const),
        pl.BlockSpec((d_ff, d), const),
    ]
    args = [x2, *parts, w_out, g.reshape(1, d), wg, wu, wd]
    if final:
        in_specs.append(pl.BlockSpec((1, d), const))
        args.append(final_g.reshape(1, d))
    return pl.pallas_call(
        functools.partial(_mix_ffn_kernel, n_parts=len(parts), ff_chunk=ff_chunk, final=final),
        grid=(m // tm,),
        in_specs=in_specs,
        out_specs=pl.BlockSpec((tm, d), lambda i: (i, 0)),
        out_shape=jax.ShapeDtypeStruct((m, d), F32),
        compiler_params=_params(("arbitrary",)),
        name="mix_ffn",
    )(*args)


def _branch_geometry(t, window, dilation, q_blk):
    seq = t // dilation
    half = window // (2 * dilation)
    if seq <= 2 * q_blk:
        q_blk = seq
    k_win = min(q_blk + 2 * half, seq)
    assert seq % q_blk == 0 and q_blk % HEAD_DIM == 0
    assert q_blk == 2 * half or seq == q_blk
    return seq, half, q_blk, k_win


def _dilated_kernel(slope_ref, q_ref, k_ref, v_ref, y_ref, so_ref, sl_ref, *bias_refs, q_blk, group):
    hp = pl.program_id(1)
    t = q_ref.shape[1]
    order = sorted(range(len(DILATED_PAIRS)), key=lambda i: DILATED_PAIRS[i][1])

    for step, bi in enumerate(order):
        window, d = DILATED_PAIRS[bi]
        seq, half, qb_rows, k_win = _branch_geometry(t, window, d, q_blk)
        bias_ref = bias_refs[bi]
        first, last = step == 0, step == len(order) - 1

        row = lax.broadcasted_iota(jnp.int32, (2 * qb_rows, k_win), 0)
        col = lax.broadcasted_iota(jnp.int32, (2 * qb_rows, k_win), 1)
        second = row >= qb_rows
        rel = col - jnp.where(second, row - qb_rows, row)
        slope2 = jnp.where(second, slope_ref[2 * hp + 1], slope_ref[2 * hp]) * (LOG2E * d)
        for si in range(bias_ref.shape[0]):
            dist = jnp.abs(rel - si * half)
            bias_ref[si] = jnp.where(dist <= half, -slope2 * dist.astype(F32), NEG_BIG)

        lane = lax.broadcasted_iota(jnp.int32, (qb_rows, LANES), 1)
        first_head = lane < HEAD_DIM
        n_blk = seq // qb_rows

        def tile_rows(idx, d=d, half=half, seq=seq, qb_rows=qb_rows, k_win=k_win, n_blk=n_blk):
            r = lax.div(idx, jnp.int32(n_blk))
            q0 = lax.rem(idx, jnp.int32(n_blk)) * qb_rows
            ws = jnp.clip(q0 - half, 0, seq - k_win)
            if d == 1:
                rows_q = pl.ds(pl.multiple_of(q0, qb_rows), qb_rows)
                rows_k = pl.ds(pl.multiple_of(ws, HEAD_DIM), k_win)
            else:
                rows_q = pl.ds(r + d * q0, qb_rows, stride=d)
                rows_k = pl.ds(r + d * ws, k_win, stride=d)
            return rows_q, rows_k, lax.div(q0 - ws, jnp.int32(half))

        def group_body(g, carry, group, d=d, qb_rows=qb_rows, bias_ref=bias_ref, first=first, last=last,
                       first_head=first_head, tile_rows=tile_rows):
            tiles = []
            for i in range(group):
                rows_q, rows_k, si = tile_rows(g * group + i)
                q = q_ref[0, rows_q, :].astype(BF16)
                zero = jnp.zeros_like(q)
                qq = jnp.concatenate([jnp.where(first_head, q, zero), jnp.where(first_head, zero, q)], axis=0)
                kw = k_ref[0, rows_k, :].astype(BF16)
                s = _dot_nt(qq, kw) + bias_ref[si]
                tiles.append((rows_q, rows_k, s))
            soft = []
            for rows_q, rows_k, s in tiles:
                mx = jnp.max(s, axis=-1, keepdims=True)
                p = jnp.exp2(s - mx)
                den = jnp.sum(p, axis=-1, keepdims=True)
                soft.append((rows_q, rows_k, p.astype(BF16), den, mx + jnp.log2(den)))
            for rows_q, rows_k, p, den, lse2 in soft:
                vw = v_ref[0, rows_k, :].astype(BF16)
                o2 = _dot(p, vw) / den
                o = jnp.where(first_head, o2[0:qb_rows], o2[qb_rows:2 * qb_rows])
                lse = jnp.where(first_head, lse2[0:qb_rows], lse2[qb_rows:2 * qb_rows])
                if not first:
                    o_prev = so_ref[rows_q, :]
                    l_prev = sl_ref[rows_q, :]
                    top = jnp.maximum(l_prev, lse)
                    w_prev = jnp.exp2(l_prev - top)
                    w_cur = jnp.exp2(lse - top)
                    tot = w_prev + w_cur
                    o = (o_prev * w_prev + o * w_cur) / tot
                    lse = top + jnp.log2(tot)
                so_ref[rows_q, :] = o
                if not last:
                    sl_ref[rows_q, :] = lse
            return carry

        n_total = d * n_blk
        grp = math.gcd(n_total, max(1, group * q_blk // qb_rows))
        lax.fori_loop(0, n_total // grp, functools.partial(group_body, group=grp), 0)

    for c in range(0, t, OUT_CHUNK):
        y_ref[0, c:c + OUT_CHUNK, :] = so_ref[c:c + OUT_CHUNK, :].astype(y_ref.dtype)


def _dilated_mixture(slopes, qkv_a, *, q_blk=128, group=8):
    b, t, _ = qkv_a.shape
    ab = W_A // LANES
    seq_blk = (1, t, LANES)
    bias_scratch = []
    for window, d in DILATED_PAIRS:
        seq, half, qb_rows, k_win = _branch_geometry(t, window, d, q_blk)
        n_shift = 1 if seq == qb_rows else 3
        bias_scratch.append(pltpu.VMEM((n_shift, 2 * qb_rows, k_win), F32))
    return pl.pallas_call(
        functools.partial(_dilated_kernel, q_blk=q_blk, group=group),
        grid=(b, ab),
        in_specs=[
            pl.BlockSpec(memory_space=pltpu.SMEM),
            pl.BlockSpec(seq_blk, lambda bi, hp: (bi, 0, hp)),
            pl.BlockSpec(seq_blk, lambda bi, hp: (bi, 0, ab + hp)),
            pl.BlockSpec(seq_blk, lambda bi, hp: (bi, 0, 2 * ab + hp)),
        ],
        out_specs=pl.BlockSpec(seq_blk, lambda bi, hp: (bi, 0, hp)),
        out_shape=jax.ShapeDtypeStruct((b, t, W_A), BF16),
        scratch_shapes=[pltpu.VMEM((t, LANES), F32), pltpu.VMEM((t, LANES), F32)] + bias_scratch,
        compiler_params=_params(("arbitrary",) * 2),
        name="dilated_mixture",
    )(slopes, qkv_a, qkv_a, qkv_a)


def _natten_kernel(q_ref, k_ref, v_ref, tbl_ref, y_ref, *, rows, group):
    n_keys = NA_ROWS * GRID_W
    lane = lax.broadcasted_iota(jnp.int32, (GRID_W, LANES), 1)
    first_head = lane < HEAD_DIM

    def group_body(g, carry):
        tiles = []
        for i in range(group):
            r = g * group + i
            rs = jnp.clip(r - NA_ROWS // 2, 0, rows - NA_ROWS)
            q0 = pl.multiple_of(r * GRID_W, GRID_W)
            k0 = pl.multiple_of(rs * GRID_W, GRID_W)
            q = q_ref[0, pl.ds(q0, GRID_W), :]
            zero = jnp.zeros_like(q)
            qq = jnp.concatenate([jnp.where(first_head, q, zero), jnp.where(first_head, zero, q)], axis=0)
            s = _dot_nt(qq, k_ref[0, pl.ds(k0, n_keys), :]) + tbl_ref[0, r - rs]
            tiles.append((q0, k0, s))
        soft = []
        for q0, k0, s in tiles:
            mx = jnp.max(s, axis=-1, keepdims=True)
            p = jnp.exp2(s - mx)
            soft.append((q0, k0, p.astype(BF16), jnp.sum(p, axis=-1, keepdims=True)))
        for q0, k0, p, den in soft:
            o2 = _dot(p, v_ref[0, pl.ds(k0, n_keys), :]) / den
            o = jnp.where(first_head, o2[0:GRID_W], o2[GRID_W:2 * GRID_W])
            y_ref[0, pl.ds(q0, GRID_W), :] = o.astype(y_ref.dtype)
        return carry

    lax.fori_loop(0, rows // group, group_body, 0)


def _natten_bias_table(rpb):
    kh = NA_ROWS
    n_h = rpb.shape[0]
    c = np.arange(GRID_W)[:, None]
    kc = np.arange(GRID_W)[None, :]
    wc = np.clip(c - NA_COLS // 2, 0, GRID_W - NA_COLS)
    col_valid = (kc >= wc) & (kc < wc + NA_COLS)
    col_idx = kc - c + NA_COLS - 1
    pick = (col_idx[None] == np.arange(2 * NA_COLS - 1)[:, None, None]) & col_valid[None]
    band = jnp.einsum("hrm,mck->hrck", rpb.astype(F32) * LOG2E, jnp.asarray(pick, F32),
                      precision=lax.Precision.HIGHEST)
    band = jnp.where(col_valid[None, None], band, NEG_BIG)
    tbl = jnp.stack([band[:, kh - 1 - dl:2 * kh - 1 - dl] for dl in range(kh)], axis=1)
    tbl = tbl.reshape(n_h // 2, 2, kh, kh, GRID_W, GRID_W)
    tbl = tbl.transpose(0, 2, 1, 4, 3, 5)
    return tbl.reshape(n_h // 2, kh, 2 * GRID_W, kh * GRID_W)


def _natten(qkv_b, rpb, *, group=8):
    b, t, _ = qkv_b.shape
    rows = t // GRID_W
    assert rows >= NA_ROWS and t % GRID_W == 0 and rows % group == 0
    tbl = _natten_bias_table(rpb)
    bb = W_B // LANES
    seq_blk = (1, t, LANES)
    return pl.pallas_call(
        functools.partial(_natten_kernel, rows=rows, group=group),
        grid=(b, bb),
        in_specs=[
            pl.BlockSpec(seq_blk, lambda bi, hp: (bi, 0, hp)),
            pl.BlockSpec(seq_blk, lambda bi, hp: (bi, 0, bb + hp)),
            pl.BlockSpec(seq_blk, lambda bi, hp: (bi, 0, 2 * bb + hp)),
            pl.BlockSpec((1, NA_ROWS, 2 * GRID_W, NA_ROWS * GRID_W), lambda bi, hp: (hp, 0, 0, 0)),
        ],
        out_specs=pl.BlockSpec(seq_blk, lambda bi, hp: (bi, 0, hp)),
        out_shape=jax.ShapeDtypeStruct((b, t, W_B), BF16),
        compiler_params=_params(("arbitrary",) * 2),
        name="natten",
    )(qkv_b, qkv_b, qkv_b, tbl)


def _diff_attn_kernel(slope_ref, q_ref, k_ref, v_ref, lq1_ref, lk1_ref, lq2_ref, lk2_ref,
                      subln_ref, y_ref, qq_ref, rel_ref, vt_ref, s_ref, mx_ref,
                      m_ref, acc_ref, *, tq, tk, n_sub, lam_init):
    h = pl.program_id(1)
    seq = k_ref.shape[1]
    n_kv = seq // tk
    slope2 = slope_ref[h] * LOG2E

    key = lax.broadcasted_iota(jnp.int32, (tk, 2 * tq), 0)
    qry = lax.broadcasted_iota(jnp.int32, (tk, 2 * tq), 1)
    qry = jnp.where(qry >= tq, qry - tq, qry)
    rel = (key - qry).astype(F32) * slope2
    rel_ref[0] = rel
    rel_ref[1] = -rel
    ones_row = lax.broadcasted_iota(jnp.int32, (DEN_ROWS, tk), 0) == 0
    for c in range(n_kv):
        vt_ref[c, 0:LANES, :] = v_ref[0, c * tk:(c + 1) * tk, :].astype(F32).T.astype(BF16)
        vt_ref[c, LANES:LANES + DEN_ROWS, :] = jnp.where(ones_row, 1.0, 0.0).astype(BF16)

    lam = (jnp.exp(jnp.sum(lq1_ref[...] * lk1_ref[...], axis=-1, keepdims=True))
           - jnp.exp(jnp.sum(lq2_ref[...] * lk2_ref[...], axis=-1, keepdims=True)) + lam_init)

    def tile_fns(u, first_tile):
        i0 = pl.multiple_of((first_tile + u) * tq, tq)
        diag = i0 // tk

        def block_offset(j):
            return (j * tk - i0).astype(F32) * slope2

        def produce(j, slot, overlapping):
            j0 = pl.multiple_of(j * tk, tk)
            s = _dot_nt(k_ref[0, pl.ds(j0, tk), :], qq_ref[u])
            if overlapping:
                s = s - jnp.abs(rel_ref[0] + block_offset(j))
            else:
                s = s + rel_ref[(j > diag).astype(jnp.int32)]
            s_ref[u, slot] = s
            mx_ref[u, slot] = jnp.max(s, axis=0, keepdims=True)

        def consume(j, slot, overlapping):
            if overlapping:
                shift = jnp.zeros((), F32)
            else:
                off = block_offset(j)
                shift = jnp.where(j > diag, -off, off)
            m_prev = m_ref[u]
            m_new = jnp.maximum(m_prev, mx_ref[u, slot] + shift)
            alpha = jnp.exp2(m_prev - m_new)
            p = jnp.exp2((s_ref[u, slot] - (m_new - shift)).astype(BF16))
            acc_ref[u] = alpha * acc_ref[u] + _dot(vt_ref[j], p)
            m_ref[u] = m_new

        def block_at(t):
            if t == 0:
                return diag
            return (t - 1) + ((t - 1) >= diag).astype(jnp.int32)

        def start():
            q = q_ref[0, pl.ds(i0, tq), :]
            lane = lax.broadcasted_iota(jnp.int32, (tq, LANES), 1)
            zero = jnp.zeros_like(q)
            qq_ref[u, 0:tq, :] = jnp.where(lane < HEAD_DIM, q, zero)
            qq_ref[u, tq:2 * tq, :] = jnp.where(lane < HEAD_DIM, zero, q)
            m_ref[u] = jnp.full(m_ref.shape[1:], NEG_BIG, F32)
            acc_ref[u] = jnp.zeros(acc_ref.shape[1:], F32)
            produce(block_at(0), 0, True)

        def middle():
            for t in range(n_kv):
                if t + 1 < n_kv:
                    produce(block_at(t + 1), (t + 1) % 2, False)
                consume(block_at(t), t % 2, t == 0)

        def finish():
            o = acc_ref[u, 0:LANES, :] / acc_ref[u, LANES:LANES + 1, :]
            y = (o[:, 0:tq] - lam * o[:, tq:2 * tq]).T
            y = _rms(y, subln_ref[...], SUBLN_EPS) * (1.0 - lam_init)
            y_ref[0, pl.ds(i0, tq), :] = y.astype(y_ref.dtype)

        return start, middle, finish

    def tile_group(g, carry):
        fns = [tile_fns(u, g * n_sub) for u in range(n_sub)]
        fns[0][0]()
        for u in range(n_sub):
            fns[u][1]()
            if u + 1 < n_sub:
                fns[u + 1][0]()
            fns[u][2]()
        return carry

    lax.fori_loop(0, seq // (n_sub * tq), tile_group, 0)


def _diff_attn(slopes, qkv, lq1, lk1, lq2, lk2, subln, lam_init, *, tq=256, tk=512, n_sub=16):
    b, t, n_cols = qkv.shape
    nh = n_cols // (3 * LANES)
    n_sub = math.gcd(n_sub, t // tq)
    assert t % tk == 0 and tk % tq == 0
    vec = lambda a: a.reshape(1, -1).astype(F32)
    small = lambda n: pl.BlockSpec((1, n), lambda bi, h: (0, 0))
    return pl.pallas_call(
        functools.partial(_diff_attn_kernel, tq=tq, tk=tk, n_sub=n_sub, lam_init=lam_init),
        grid=(b, nh),
        in_specs=[
            pl.BlockSpec(memory_space=pltpu.SMEM),
            pl.BlockSpec((1, t, LANES), lambda bi, h: (bi, 0, h)),
            pl.BlockSpec((1, t, LANES), lambda bi, h: (bi, 0, nh + h)),
            pl.BlockSpec((1, t, LANES), lambda bi, h: (bi, 0, 2 * nh + h)),
            small(HEAD_DIM), small(HEAD_DIM), small(HEAD_DIM), small(HEAD_DIM),
            small(2 * HEAD_DIM),
        ],
        out_specs=pl.BlockSpec((1, t, LANES), lambda bi, h: (bi, 0, h)),
        out_shape=jax.ShapeDtypeStruct((b, t, nh * LANES), BF16),
        scratch_shapes=[
            pltpu.VMEM((n_sub, 2 * tq, LANES), BF16),
            pltpu.VMEM((2, tk, 2 * tq), F32),
            pltpu.VMEM((t // tk, LANES + DEN_ROWS, tk), BF16),
            pltpu.VMEM((n_sub, 2, tk, 2 * tq), F32),
            pltpu.VMEM((n_sub, 2, 1, 2 * tq), F32),
            pltpu.VMEM((n_sub, 1, 2 * tq), F32),
            pltpu.VMEM((n_sub, LANES + DEN_ROWS, 2 * tq), F32),
        ],
        compiler_params=_params(("arbitrary",) * 2),
        name="diff_attn",
    )(slopes, qkv, qkv, qkv, vec(lq1), vec(lk1), vec(lq2), vec(lk2), vec(subln))


def _alibi_slopes(n):
    return jnp.exp2(-8.0 * jnp.arange(1, n + 1, dtype=F32) / n)


def _lambda_init(layer):
    return 0.8 - 0.6 * math.exp(-0.3 * layer)


def kernel(x, a0_norm, a0_w_in, a0_w_out, a0_rpb, f0_norm, f0_w_gate, f0_w_up, f0_w_down,
           a1_norm, a1_w_qkv, a1_w_out, a1_lam_q1, a1_lam_k1, a1_lam_q2, a1_lam_k2, a1_subln,
           f1_norm, f1_w_gate, f1_w_up, f1_w_down, final_norm):
    b, t, d = x.shape
    x2 = x.reshape(b * t, d)
    qscale = HEAD_DIM ** -0.5 * LOG2E
    bf = lambda w: w.astype(BF16)

    cs0 = np.ones((3 * W_A + 3 * W_B,), np.float32)
    cs0[0:W_A] = qscale
    cs0[3 * W_A:3 * W_A + W_B] = qscale
    qkv_a, qkv_b = _norm_proj(x2, a0_norm, bf(a0_w_in), jnp.asarray(cs0),
                              ((3 * W_A, F32), (3 * W_B, BF16)))
    ya = _dilated_mixture(_alibi_slopes(N_HEADS_A), qkv_a.reshape(b, t, 3 * W_A)).reshape(b * t, W_A)
    yb = _natten(qkv_b.reshape(b, t, 3 * W_B), a0_rpb).reshape(b * t, W_B)
    x2 = _mix_ffn(x2, [ya, yb], bf(a0_w_out), f0_norm, bf(f0_w_gate), bf(f0_w_up), bf(f0_w_down))

    wc = a1_w_qkv.shape[1] // 3
    cs1 = np.ones((3 * wc,), np.float32)
    cs1[0:wc] = qscale
    (qkv1,) = _norm_proj(x2, a1_norm, bf(a1_w_qkv), jnp.asarray(cs1), ((3 * wc, BF16),))
    qkv1 = qkv1.reshape(b, t, 3 * wc)
    n_heads_c = wc // (2 * HEAD_DIM)
    y = _diff_attn(_alibi_slopes(n_heads_c), qkv1, a1_lam_q1, a1_lam_k1, a1_lam_q2, a1_lam_k2,
                   a1_subln, _lambda_init(1)).reshape(b * t, wc)
    out = _mix_ffn(x2, [y], bf(a1_w_out), f1_norm, bf(f1_w_gate), bf(f1_w_up), bf(f1_w_down),
                   final_norm)
    return out.reshape(b, t, d)
```

```python
import functools
import math

import numpy as np
import jax
import jax.numpy as jnp
from jax import lax
from jax.experimental import pallas as pl
from jax.experimental.pallas import tpu as pltpu

HEAD_DIM = 64
N_HEADS_A = 8
N_HEADS_B = 8
W_A = N_HEADS_A * HEAD_DIM
W_B = N_HEADS_B * HEAD_DIM
DILATED_PAIRS = ((128, 1), (512, 4), (2048, 16))
GRID_W = 64
NA_ROWS = 8
NA_COLS = 16
RMS_EPS = 1e-6
SUBLN_EPS = 1e-5
LOG2E = math.log2(math.e)

LANES = 128
OUT_CHUNK = 512
DEN_ROWS = 16
V7X_VMEM_LIMIT_BYTES = 56 * 1024 * 1024

NEG_BIG = -1e30

BF16 = jnp.bfloat16
F32 = jnp.float32


def _params(semantics):
    return pltpu.CompilerParams(dimension_semantics=semantics,
                                vmem_limit_bytes=V7X_VMEM_LIMIT_BYTES)


def _rms(x, g, eps):
    ms = jnp.mean(x * x, axis=-1, keepdims=True)
    return (x * lax.rsqrt(ms + eps)) * g


def _dot_nt(a, b):
    return lax.dot_general(a, b, (((1,), (1,)), ((), ())), preferred_element_type=F32)


def _dot(a, b):
    return jnp.dot(a, b, preferred_element_type=F32)


def _norm_proj_kernel(x_ref, g_ref, w_ref, cs_ref, *o_refs, n_chunk):
    h = _rms(x_ref[...], g_ref[...], RMS_EPS).astype(BF16)
    col = 0
    for o_ref in o_refs:
        for c in range(0, o_ref.shape[1], n_chunk):
            acc = _dot(h, w_ref[:, col + c:col + c + n_chunk])
            o_ref[:, c:c + n_chunk] = (acc * cs_ref[:, col + c:col + c + n_chunk]).astype(o_ref.dtype)
        col += o_ref.shape[1]


def _norm_proj(x2, g, w, colscale, outs, *, tm=1024, n_chunk=512):
    m, d = x2.shape
    n = w.shape[1]
    assert sum(c for c, _ in outs) == n and all(c % n_chunk == 0 for c, _ in outs)
    return pl.pallas_call(
        functools.partial(_norm_proj_kernel, n_chunk=n_chunk),
        grid=(m // tm,),
        in_specs=[
            pl.BlockSpec((tm, d), lambda i: (i, 0)),
            pl.BlockSpec((1, d), lambda i: (0, 0)),
            pl.BlockSpec((d, n), lambda i: (0, 0)),
            pl.BlockSpec((1, n), lambda i: (0, 0)),
        ],
        out_specs=[pl.BlockSpec((tm, c), lambda i: (i, 0)) for c, _ in outs],
        out_shape=[jax.ShapeDtypeStruct((m, c), dt) for c, dt in outs],
        compiler_params=_params(("arbitrary",)),
        name="norm_proj",
    )(x2, g.reshape(1, d), w, colscale.reshape(1, n))


def _mix_ffn_kernel(*refs, n_parts, ff_chunk, final):
    x_ref = refs[0]
    part_refs = refs[1:1 + n_parts]
    wo_ref, g_ref, wg_ref, wu_ref, wd_ref = refs[1 + n_parts:6 + n_parts]
    fg_ref = refs[6 + n_parts] if final else None
    o_ref = refs[-1]

    y = jnp.concatenate([p_ref[...] for p_ref in part_refs], axis=1)
    x = x_ref[...] + _dot(y, wo_ref[...])
    h = _rms(x, g_ref[...], RMS_EPS).astype(BF16)
    d_ff = wg_ref.shape[1]
    acc = x
    for c in range(0, d_ff, ff_chunk):
        gate = _dot(h, wg_ref[:, c:c + ff_chunk])
        up = _dot(h, wu_ref[:, c:c + ff_chunk])
        a = (gate * jax.nn.sigmoid(gate) * up).astype(BF16)
        acc = acc + _dot(a, wd_ref[c:c + ff_chunk, :])
    if final:
        acc = _rms(acc, fg_ref[...], RMS_EPS)
    o_ref[...] = acc


def _mix_ffn(x2, parts, w_out, g, wg, wu, wd, final_g=None, *, tm=1024, ff_chunk=256):
    m, d = x2.shape
    d_ff = wg.shape[1]
    final = final_g is not None
    const = lambda i: (0, 0)
    in_specs = [pl.BlockSpec((tm, d), lambda i: (i, 0))]
    in_specs += [pl.BlockSpec((tm, p.shape[1]), lambda i: (i, 0)) for p in parts]
    in_specs += [
        pl.BlockSpec((d, d), const),
        pl.BlockSpec((1, d), const),
        pl.BlockSpec((d, d_ff), const),
        pl.BlockSpec((d, d_ff), const),
        pl.BlockSpec((d_ff, d), const),
    ]
    args = [x2, *parts, w_out, g.reshape(1, d), wg, wu, wd]
    if final:
        in_specs.append(pl.BlockSpec((1, d), const))
        args.append(final_g.reshape(1, d))
    return pl.pallas_call(
        functools.partial(_mix_ffn_kernel, n_parts=len(parts), ff_chunk=ff_chunk, final=final),
        grid=(m // tm,),
        in_specs=in_specs,
        out_specs=pl.BlockSpec((tm, d), lambda i: (i, 0)),
        out_shape=jax.ShapeDtypeStruct((m, d), F32),
        compiler_params=_params(("arbitrary",)),
        name="mix_ffn",
    )(*args)


def _branch_geometry(t, window, dilation, q_blk):
    seq = t // dilation
    half = window // (2 * dilation)
    if seq <= 2 * q_blk:
        q_blk = seq
    k_win = min(q_blk + 2 * half, seq)
    assert seq % q_blk == 0 and q_blk % HEAD_DIM == 0
    assert q_blk == 2 * half or seq == q_blk
    return seq, half, q_blk, k_win


def _dilated_kernel(slope_ref, q_ref, k_ref, v_ref, y_ref, so_ref, sl_ref, *bias_refs, q_blk, group):
    hp = pl.program_id(1)
    t = q_ref.shape[1]
    order = sorted(range(len(DILATED_PAIRS)), key=lambda i: DILATED_PAIRS[i][1])

    for step, bi in enumerate(order):
        window, d = DILATED_PAIRS[bi]
        seq, half, qb_rows, k_win = _branch_geometry(t, window, d, q_blk)
        bias_ref = bias_refs[bi]
        first, last = step == 0, step == len(order) - 1

        row = lax.broadcasted_iota(jnp.int32, (2 * qb_rows, k_win), 0)
        col = lax.broadcasted_iota(jnp.int32, (2 * qb_rows, k_win), 1)
        second = row >= qb_rows
        rel = col - jnp.where(second, row - qb_rows, row)
        slope2 = jnp.where(second, slope_ref[2 * hp + 1], slope_ref[2 * hp]) * (LOG2E * d)
        for si in range(bias_ref.shape[0]):
            dist = jnp.abs(rel - si * half)
            bias_ref[si] = jnp.where(dist <= half, -slope2 * dist.astype(F32), NEG_BIG)

        lane = lax.broadcasted_iota(jnp.int32, (qb_rows, LANES), 1)
        first_head = lane < HEAD_DIM
        n_blk = seq // qb_rows

        def tile_rows(idx, d=d, half=half, seq=seq, qb_rows=qb_rows, k_win=k_win, n_blk=n_blk):
            r = lax.div(idx, jnp.int32(n_blk))
            q0 = lax.rem(idx, jnp.int32(n_blk)) * qb_rows
            ws = jnp.clip(q0 - half, 0, seq - k_win)
            if d == 1:
                rows_q = pl.ds(pl.multiple_of(q0, qb_rows), qb_rows)
                rows_k = pl.ds(pl.multiple_of(ws, HEAD_DIM), k_win)
            else:
                rows_q = pl.ds(r + d * q0, qb_rows, stride=d)
                rows_k = pl.ds(r + d * ws, k_win, stride=d)
            return rows_q, rows_k, lax.div(q0 - ws, jnp.int32(half))

        def group_body(g, carry, group, d=d, qb_rows=qb_rows, bias_ref=bias_ref, first=first, last=last,
                       first_head=first_head, tile_rows=tile_rows):
            tiles = []
            for i in range(group):
                rows_q, rows_k, si = tile_rows(g * group + i)
                q = q_ref[0, rows_q, :].astype(BF16)
                zero = jnp.zeros_like(q)
                qq = jnp.concatenate([jnp.where(first_head, q, zero), jnp.where(first_head, zero, q)], axis=0)
                kw = k_ref[0, rows_k, :].astype(BF16)
                s = _dot_nt(qq, kw) + bias_ref[si]
                tiles.append((rows_q, rows_k, s))
            soft = []
            for rows_q, rows_k, s in tiles:
                mx = jnp.max(s, axis=-1, keepdims=True)
                p = jnp.exp2(s - mx)
                den = jnp.sum(p, axis=-1, keepdims=True)
                soft.append((rows_q, rows_k, p.astype(BF16), den, mx + jnp.log2(den)))
            for rows_q, rows_k, p, den, lse2 in soft:
                vw = v_ref[0, rows_k, :].astype(BF16)
                o2 = _dot(p, vw) / den
                o = jnp.where(first_head, o2[0:qb_rows], o2[qb_rows:2 * qb_rows])
                lse = jnp.where(first_head, lse2[0:qb_rows], lse2[qb_rows:2 * qb_rows])
                if not first:
                    o_prev = so_ref[rows_q, :]
                    l_prev = sl_ref[rows_q, :]
                    top = jnp.maximum(l_prev, lse)
                    w_prev = jnp.exp2(l_prev - top)
                    w_cur = jnp.exp2(lse - top)
                    tot = w_prev + w_cur
                    o = (o_prev * w_prev + o * w_cur) / tot
                    lse = top + jnp.log2(tot)
                so_ref[rows_q, :] = o
                if not last:
                    sl_ref[rows_q, :] = lse
            return carry

        n_total = d * n_blk
        grp = math.gcd(n_total, max(1, group * q_blk // qb_rows))
        lax.fori_loop(0, n_total // grp, functools.partial(group_body, group=grp), 0)

    for c in range(0, t, OUT_CHUNK):
        y_ref[0, c:c + OUT_CHUNK, :] = so_ref[c:c + OUT_CHUNK, :].astype(y_ref.dtype)


def _dilated_mixture(slopes, qkv_a, *, q_blk=128, group=8):
    b, t, _ = qkv_a.shape
    ab = W_A // LANES
    seq_blk = (1, t, LANES)
    bias_scratch = []
    for window, d in DILATED_PAIRS:
        seq, half, qb_rows, k_win = _branch_geometry(t, window, d, q_blk)
        n_shift = 1 if seq == qb_rows else 3
        bias_scratch.append(pltpu.VMEM((n_shift, 2 * qb_rows, k_win), F32))
    return pl.pallas_call(
        functools.partial(_dilated_kernel, q_blk=q_blk, group=group),
        grid=(b, ab),
        in_specs=[
            pl.BlockSpec(memory_space=pltpu.SMEM),
            pl.BlockSpec(seq_blk, lambda bi, hp: (bi, 0, hp)),
            pl.BlockSpec(seq_blk, lambda bi, hp: (bi, 0, ab + hp)),
            pl.BlockSpec(seq_blk, lambda bi, hp: (bi, 0, 2 * ab + hp)),
        ],
        out_specs=pl.BlockSpec(seq_blk, lambda bi, hp: (bi, 0, hp)),
        out_shape=jax.ShapeDtypeStruct((b, t, W_A), BF16),
        scratch_shapes=[pltpu.VMEM((t, LANES), F32), pltpu.VMEM((t, LANES), F32)] + bias_scratch,
        compiler_params=_params(("arbitrary",) * 2),
        name="dilated_mixture",
    )(slopes, qkv_a, qkv_a, qkv_a)


def _natten_kernel(q_ref, k_ref, v_ref, tbl_ref, y_ref, *, rows, group):
    n_keys = NA_ROWS * GRID_W
    lane = lax.broadcasted_iota(jnp.int32, (GRID_W, LANES), 1)
    first_head = lane < HEAD_DIM

    def group_body(g, carry):
        tiles = []
        for i in range(group):
            r = g * group + i
            rs = jnp.clip(r - NA_ROWS // 2, 0, rows - NA_ROWS)
            q0 = pl.multiple_of(r * GRID_W, GRID_W)
            k0 = pl.multiple_of(rs * GRID_W, GRID_W)
            q = q_ref[0, pl.ds(q0, GRID_W), :]
            zero = jnp.zeros_like(q)
            qq = jnp.concatenate([jnp.where(first_head, q, zero), jnp.where(first_head, zero, q)], axis=0)
            s = _dot_nt(qq, k_ref[0, pl.ds(k0, n_keys), :]) + tbl_ref[0, r - rs]
            tiles.append((q0, k0, s))
        soft = []
        for q0, k0, s in tiles:
            mx = jnp.max(s, axis=-1, keepdims=True)
            p = jnp.exp2(s - mx)
            soft.append((q0, k0, p.astype(BF16), jnp.sum(p, axis=-1, keepdims=True)))
        for q0, k0, p, den in soft:
            o2 = _dot(p, v_ref[0, pl.ds(k0, n_keys), :]) / den
            o = jnp.where(first_head, o2[0:GRID_W], o2[GRID_W:2 * GRID_W])
            y_ref[0, pl.ds(q0, GRID_W), :] = o.astype(y_ref.dtype)
        return carry

    lax.fori_loop(0, rows // group, group_body, 0)


def _natten_bias_table(rpb):
    kh = NA_ROWS
    n_h = rpb.shape[0]
    c = np.arange(GRID_W)[:, None]
    kc = np.arange(GRID_W)[None, :]
    wc = np.clip(c - NA_COLS // 2, 0, GRID_W - NA_COLS)
    col_valid = (kc >= wc) & (kc < wc + NA_COLS)
    col_idx = kc - c + NA_COLS - 1
    pick = (col_idx[None] == np.arange(2 * NA_COLS - 1)[:, None, None]) & col_valid[None]
    band = jnp.einsum("hrm,mck->hrck", rpb.astype(F32) * LOG2E, jnp.asarray(pick, F32),
                      precision=lax.Precision.HIGHEST)
    band = jnp.where(col_valid[None, None], band, NEG_BIG)
    tbl = jnp.stack([band[:, kh - 1 - dl:2 * kh - 1 - dl] for dl in range(kh)], axis=1)
    tbl = tbl.reshape(n_h // 2, 2, kh, kh, GRID_W, GRID_W)
    tbl = tbl.transpose(0, 2, 1, 4, 3, 5)
    return tbl.reshape(n_h // 2, kh, 2 * GRID_W, kh * GRID_W)


def _natten(qkv_b, rpb, *, group=16):
    b, t, _ = qkv_b.shape
    rows = t // GRID_W
    assert rows >= NA_ROWS and t % GRID_W == 0 and rows % group == 0
    tbl = _natten_bias_table(rpb)
    bb = W_B // LANES
    seq_blk = (1, t, LANES)
    return pl.pallas_call(
        functools.partial(_natten_kernel, rows=rows, group=group),
        grid=(b, bb),
        in_specs=[
            pl.BlockSpec(seq_blk, lambda bi, hp: (bi, 0, hp)),
            pl.BlockSpec(seq_blk, lambda bi, hp: (bi, 0, bb + hp)),
            pl.BlockSpec(seq_blk, lambda bi, hp: (bi, 0, 2 * bb + hp)),
            pl.BlockSpec((1, NA_ROWS, 2 * GRID_W, NA_ROWS * GRID_W), lambda bi, hp: (hp, 0, 0, 0)),
        ],
        out_specs=pl.BlockSpec(seq_blk, lambda bi, hp: (bi, 0, hp)),
        out_shape=jax.ShapeDtypeStruct((b, t, W_B), BF16),
        compiler_params=_params(("arbitrary",) * 2),
        name="natten",
    )(qkv_b, qkv_b, qkv_b, tbl)


def _diff_attn_kernel(slope_ref, q_ref, k_ref, v_ref, lq1_ref, lk1_ref, lq2_ref, lk2_ref,
                      subln_ref, y_ref, qq_ref, rel_ref, vt_ref, s_ref, mx_ref,
                      m_ref, acc_ref, *, tq, tk, n_sub, lam_init):
    h = pl.program_id(1)
    seq = k_ref.shape[1]
    n_kv = seq // tk
    slope2 = slope_ref[h] * LOG2E

    key = lax.broadcasted_iota(jnp.int32, (tk, 2 * tq), 0)
    qry = lax.broadcasted_iota(jnp.int32, (tk, 2 * tq), 1)
    qry = jnp.where(qry >= tq, qry - tq, qry)
    rel = (key - qry).astype(F32) * slope2
    rel_ref[0] = rel
    rel_ref[1] = -rel
    ones_row = lax.broadcasted_iota(jnp.int32, (DEN_ROWS, tk), 0) == 0
    for c in range(n_kv):
        vt_ref[c, 0:LANES, :] = v_ref[0, c * tk:(c + 1) * tk, :].astype(F32).T.astype(BF16)
        vt_ref[c, LANES:LANES + DEN_ROWS, :] = jnp.where(ones_row, 1.0, 0.0).astype(BF16)

    lam = (jnp.exp(jnp.sum(lq1_ref[...] * lk1_ref[...], axis=-1, keepdims=True))
           - jnp.exp(jnp.sum(lq2_ref[...] * lk2_ref[...], axis=-1, keepdims=True)) + lam_init)

    def tile_fns(u, first_tile):
        i0 = pl.multiple_of((first_tile + u) * tq, tq)
        diag = i0 // tk

        def block_offset(j):
            return (j * tk - i0).astype(F32) * slope2

        def produce(j, slot, overlapping):
            j0 = pl.multiple_of(j * tk, tk)
            s = _dot_nt(k_ref[0, pl.ds(j0, tk), :], qq_ref[u])
            if overlapping:
                s = s - jnp.abs(rel_ref[0] + block_offset(j))
            else:
                s = s + rel_ref[(j > diag).astype(jnp.int32)]
            s_ref[u, slot] = s
            mx_ref[u, slot] = jnp.max(s, axis=0, keepdims=True)

        def consume(j, slot, overlapping):
            if overlapping:
                shift = jnp.zeros((), F32)
            else:
                off = block_offset(j)
                shift = jnp.where(j > diag, -off, off)
            m_prev = m_ref[u]
            m_new = jnp.maximum(m_prev, mx_ref[u, slot] + shift)
            alpha = jnp.exp2(m_prev - m_new)
            p = jnp.exp2((s_ref[u, slot] - (m_new - shift)).astype(BF16))
            acc_ref[u] = alpha * acc_ref[u] + _dot(vt_ref[j], p)
            m_ref[u] = m_new

        def block_at(t):
            if t == 0:
                return diag
            return (t - 1) + ((t - 1) >= diag).astype(jnp.int32)

        def start():
            q = q_ref[0, pl.ds(i0, tq), :]
            lane = lax.broadcasted_iota(jnp.int32, (tq, LANES), 1)
            zero = jnp.zeros_like(q)
            qq_ref[u, 0:tq, :] = jnp.where(lane < HEAD_DIM, q, zero)
            qq_ref[u, tq:2 * tq, :] = jnp.where(lane < HEAD_DIM, zero, q)
            m_ref[u] = jnp.full(m_ref.shape[1:], NEG_BIG, F32)
            acc_ref[u] = jnp.zeros(acc_ref.shape[1:], F32)
            produce(block_at(0), 0, True)

        def middle():
            for t in range(n_kv):
                if t + 1 < n_kv:
                    produce(block_at(t + 1), (t + 1) % 2, False)
                consume(block_at(t), t % 2, t == 0)

        def finish():
            o = acc_ref[u, 0:LANES, :] / acc_ref[u, LANES:LANES + 1, :]
            y = (o[:, 0:tq] - lam * o[:, tq:2 * tq]).T
            y = _rms(y, subln_ref[...], SUBLN_EPS) * (1.0 - lam_init)
            y_ref[0, pl.ds(i0, tq), :] = y.astype(y_ref.dtype)

        return start, middle, finish

    def tile_group(g, carry):
        fns = [tile_fns(u, g * n_sub) for u in range(n_sub)]
        fns[0][0]()
        for u in range(n_sub):
            fns[u][1]()
            if u + 1 < n_sub:
                fns[u + 1][0]()
            fns[u][2]()
        return carry

    lax.fori_loop(0, seq // (n_sub * tq), tile_group, 0)


def _diff_attn(slopes, qkv, lq1, lk1, lq2, lk2, subln, lam_init, *, tq=256, tk=512, n_sub=8):
    b, t, n_cols = qkv.shape
    nh = n_cols // (3 * LANES)
    n_sub = math.gcd(n_sub, t // tq)
    assert t % tk == 0 and tk % tq == 0
    vec = lambda a: a.reshape(1, -1).astype(F32)
    small = lambda n: pl.BlockSpec((1, n), lambda bi, h: (0, 0))
    return pl.pallas_call(
        functools.partial(_diff_attn_kernel, tq=tq, tk=tk, n_sub=n_sub, lam_init=lam_init),
        grid=(b, nh),
        in_specs=[
            pl.BlockSpec(memory_space=pltpu.SMEM),
            pl.BlockSpec((1, t, LANES), lambda bi, h: (bi, 0, h)),
            pl.BlockSpec((1, t, LANES), lambda bi, h: (bi, 0, nh + h)),
            pl.BlockSpec((1, t, LANES), lambda bi, h: (bi, 0, 2 * nh + h)),
            small(HEAD_DIM), small(HEAD_DIM), small(HEAD_DIM), small(HEAD_DIM),
            small(2 * HEAD_DIM),
        ],
        out_specs=pl.BlockSpec((1, t, LANES), lambda bi, h: (bi, 0, h)),
        out_shape=jax.ShapeDtypeStruct((b, t, nh * LANES), BF16),
        scratch_shapes=[
            pltpu.VMEM((n_sub, 2 * tq, LANES), BF16),
            pltpu.VMEM((2, tk, 2 * tq), F32),
            pltpu.VMEM((t // tk, LANES + DEN_ROWS, tk), BF16),
            pltpu.VMEM((n_sub, 2, tk, 2 * tq), F32),
            pltpu.VMEM((n_sub, 2, 1, 2 * tq), F32),
            pltpu.VMEM((n_sub, 1, 2 * tq), F32),
            pltpu.VMEM((n_sub, LANES + DEN_ROWS, 2 * tq), F32),
        ],
        compiler_params=_params(("arbitrary",) * 2),
        name="diff_attn",
    )(slopes, qkv, qkv, qkv, vec(lq1), vec(lk1), vec(lq2), vec(lk2), vec(subln))


def _alibi_slopes(n):
    return jnp.exp2(-8.0 * jnp.arange(1, n + 1, dtype=F32) / n)


def _lambda_init(layer):
    return 0.8 - 0.6 * math.exp(-0.3 * layer)


def kernel(x, a0_norm, a0_w_in, a0_w_out, a0_rpb, f0_norm, f0_w_gate, f0_w_up, f0_w_down,
           a1_norm, a1_w_qkv, a1_w_out, a1_lam_q1, a1_lam_k1, a1_lam_q2, a1_lam_k2, a1_subln,
           f1_norm, f1_w_gate, f1_w_up, f1_w_down, final_norm):
    b, t, d = x.shape
    x2 = x.reshape(b * t, d)
    qscale = HEAD_DIM ** -0.5 * LOG2E
    bf = lambda w: w.astype(BF16)

    cs0 = np.ones((3 * W_A + 3 * W_B,), np.float32)
    cs0[0:W_A] = qscale
    cs0[3 * W_A:3 * W_A + W_B] = qscale
    qkv_a, qkv_b = _norm_proj(x2, a0_norm, bf(a0_w_in), jnp.asarray(cs0),
                              ((3 * W_A, F32), (3 * W_B, BF16)))
    ya = _dilated_mixture(_alibi_slopes(N_HEADS_A), qkv_a.reshape(b, t, 3 * W_A)).reshape(b * t, W_A)
    yb = _natten(qkv_b.reshape(b, t, 3 * W_B), a0_rpb).reshape(b * t, W_B)
    x2 = _mix_ffn(x2, [ya, yb], bf(a0_w_out), f0_norm, bf(f0_w_gate), bf(f0_w_up), bf(f0_w_down))

    wc = a1_w_qkv.shape[1] // 3
    cs1 = np.ones((3 * wc,), np.float32)
    cs1[0:wc] = qscale
    (qkv1,) = _norm_proj(x2, a1_norm, bf(a1_w_qkv), jnp.asarray(cs1), ((3 * wc, BF16),))
    qkv1 = qkv1.reshape(b, t, 3 * wc)
    n_heads_c = wc // (2 * HEAD_DIM)
    y = _diff_attn(_alibi_slopes(n_heads_c), qkv1, a1_lam_q1, a1_lam_k1, a1_lam_q2, a1_lam_k2,
                   a1_subln, _lambda_init(1)).reshape(b * t, wc)
    out = _mix_ffn(x2, [y], bf(a1_w_out), f1_norm, bf(f1_w_gate), bf(f1_w_up), bf(f1_w_down),
                   final_norm)
    return out.reshape(b, t, d)
```

```python
import functools
import math

import numpy as np
import jax
import jax.numpy as jnp
from jax import lax
from jax.experimental import pallas as pl
from jax.experimental.pallas import tpu as pltpu

HEAD_DIM = 64
N_HEADS_A = 8
N_HEADS_B = 8
W_A = N_HEADS_A * HEAD_DIM
W_B = N_HEADS_B * HEAD_DIM
DILATED_PAIRS = ((128, 1), (512, 4), (2048, 16))
GRID_W = 64
NA_ROWS = 8
NA_COLS = 16
RMS_EPS = 1e-6
SUBLN_EPS = 1e-5
LOG2E = math.log2(math.e)

LANES = 128
OUT_CHUNK = 512
DEN_ROWS = 16
V7X_VMEM_LIMIT_BYTES = 56 * 1024 * 1024

NEG_BIG = -1e30

BF16 = jnp.bfloat16
F32 = jnp.float32


def _params(semantics):
    return pltpu.CompilerParams(dimension_semantics=semantics,
                                vmem_limit_bytes=V7X_VMEM_LIMIT_BYTES)


def _rms(x, g, eps):
    ms = jnp.mean(x * x, axis=-1, keepdims=True)
    return (x * lax.rsqrt(ms + eps)) * g


def _dot_nt(a, b):
    return lax.dot_general(a, b, (((1,), (1,)), ((), ())), preferred_element_type=F32)


def _dot(a, b):
    return jnp.dot(a, b, preferred_element_type=F32)


def _norm_proj_kernel(x_ref, g_ref, w_ref, cs_ref, *o_refs, n_chunk):
    h = _rms(x_ref[...], g_ref[...], RMS_EPS).astype(BF16)
    col = 0
    for o_ref in o_refs:
        for c in range(0, o_ref.shape[1], n_chunk):
            acc = _dot(h, w_ref[:, col + c:col + c + n_chunk])
            o_ref[:, c:c + n_chunk] = (acc * cs_ref[:, col + c:col + c + n_chunk]).astype(o_ref.dtype)
        col += o_ref.shape[1]


def _norm_proj(x2, g, w, colscale, outs, *, tm=1024, n_chunk=512):
    m, d = x2.shape
    n = w.shape[1]
    assert sum(c for c, _ in outs) == n and all(c % n_chunk == 0 for c, _ in outs)
    return pl.pallas_call(
        functools.partial(_norm_proj_kernel, n_chunk=n_chunk),
        grid=(m // tm,),
        in_specs=[
            pl.BlockSpec((tm, d), lambda i: (i, 0)),
            pl.BlockSpec((1, d), lambda i: (0, 0)),
            pl.BlockSpec((d, n), lambda i: (0, 0)),
            pl.BlockSpec((1, n), lambda i: (0, 0)),
        ],
        out_specs=[pl.BlockSpec((tm, c), lambda i: (i, 0)) for c, _ in outs],
        out_shape=[jax.ShapeDtypeStruct((m, c), dt) for c, dt in outs],
        compiler_params=_params(("arbitrary",)),
        name="norm_proj",
    )(x2, g.reshape(1, d), w, colscale.reshape(1, n))


def _mix_ffn_kernel(*refs, n_parts, ff_chunk, final):
    x_ref = refs[0]
    part_refs = refs[1:1 + n_parts]
    wo_ref, g_ref, wg_ref, wu_ref, wd_ref = refs[1 + n_parts:6 + n_parts]
    fg_ref = refs[6 + n_parts] if final else None
    o_ref = refs[-1]

    y = jnp.concatenate([p_ref[...] for p_ref in part_refs], axis=1)
    x = x_ref[...] + _dot(y, wo_ref[...])
    h = _rms(x, g_ref[...], RMS_EPS).astype(BF16)
    d_ff = wg_ref.shape[1]
    acc = x
    for c in range(0, d_ff, ff_chunk):
        gate = _dot(h, wg_ref[:, c:c + ff_chunk])
        up = _dot(h, wu_ref[:, c:c + ff_chunk])
        a = (gate * jax.nn.sigmoid(gate) * up).astype(BF16)
        acc = acc + _dot(a, wd_ref[c:c + ff_chunk, :])
    if final:
        acc = _rms(acc, fg_ref[...], RMS_EPS)
    o_ref[...] = acc


def _mix_ffn(x2, parts, w_out, g, wg, wu, wd, final_g=None, *, tm=1024, ff_chunk=256):
    m, d = x2.shape
    d_ff = wg.shape[1]
    final = final_g is not None
    const = lambda i: (0, 0)
    in_specs = [pl.BlockSpec((tm, d), lambda i: (i, 0))]
    in_specs += [pl.BlockSpec((tm, p.shape[1]), lambda i: (i, 0)) for p in parts]
    in_specs += [
        pl.BlockSpec((d, d), const),
        pl.BlockSpec((1, d), const),
        pl.BlockSpec((d, d_ff), const),
        pl.BlockSpec((d, d_ff), const),
        pl.BlockSpec((d_ff, d), const),
    ]
    args = [x2, *parts, w_out, g.reshape(1, d), wg, wu, wd]
    if final:
        in_specs.append(pl.BlockSpec((1, d), const))
        args.append(final_g.reshape(1, d))
    return pl.pallas_call(
        functools.partial(_mix_ffn_kernel, n_parts=len(parts), ff_chunk=ff_chunk, final=final),
        grid=(m // tm,),
        in_specs=in_specs,
        out_specs=pl.BlockSpec((tm, d), lambda i: (i, 0)),
        out_shape=jax.ShapeDtypeStruct((m, d), F32),
        compiler_params=_params(("arbitrary",)),
        name="mix_ffn",
    )(*args)


def _branch_geometry(t, window, dilation, q_blk):
    seq = t // dilation
    half = window // (2 * dilation)
    if seq <= 2 * q_blk:
        q_blk = seq
    k_win = min(q_blk + 2 * half, seq)
    assert seq % q_blk == 0 and q_blk % HEAD_DIM == 0
    assert q_blk == 2 * half or seq == q_blk
    return seq, half, q_blk, k_win


def _dilated_kernel(slope_ref, q_ref, k_ref, v_ref, y_ref, so_ref, sl_ref, *bias_refs, q_blk, group):
    hp = pl.program_id(1)
    t = q_ref.shape[1]
    order = sorted(range(len(DILATED_PAIRS)), key=lambda i: DILATED_PAIRS[i][1])

    for step, bi in enumerate(order):
        window, d = DILATED_PAIRS[bi]
        seq, half, qb_rows, k_win = _branch_geometry(t, window, d, q_blk)
        bias_ref = bias_refs[bi]
        first, last = step == 0, step == len(order) - 1

        row = lax.broadcasted_iota(jnp.int32, (2 * qb_rows, k_win), 0)
        col = lax.broadcasted_iota(jnp.int32, (2 * qb_rows, k_win), 1)
        second = row >= qb_rows
        rel = col - jnp.where(second, row - qb_rows, row)
        slope2 = jnp.where(second, slope_ref[2 * hp + 1], slope_ref[2 * hp]) * (LOG2E * d)
        for si in range(bias_ref.shape[0]):
            dist = jnp.abs(rel - si * half)
            bias_ref[si] = jnp.where(dist <= half, -slope2 * dist.astype(F32), NEG_BIG)

        lane = lax.broadcasted_iota(jnp.int32, (qb_rows, LANES), 1)
        first_head = lane < HEAD_DIM
        n_blk = seq // qb_rows

        def tile_rows(idx, d=d, half=half, seq=seq, qb_rows=qb_rows, k_win=k_win, n_blk=n_blk):
            r = lax.div(idx, jnp.int32(n_blk))
            q0 = lax.rem(idx, jnp.int32(n_blk)) * qb_rows
            ws = jnp.clip(q0 - half, 0, seq - k_win)
            if d == 1:
                rows_q = pl.ds(pl.multiple_of(q0, qb_rows), qb_rows)
                rows_k = pl.ds(pl.multiple_of(ws, HEAD_DIM), k_win)
            else:
                rows_q = pl.ds(r + d * q0, qb_rows, stride=d)
                rows_k = pl.ds(r + d * ws, k_win, stride=d)
            return rows_q, rows_k, lax.div(q0 - ws, jnp.int32(half))

        def group_body(g, carry, group, d=d, qb_rows=qb_rows, bias_ref=bias_ref, first=first, last=last,
                       first_head=first_head, tile_rows=tile_rows):
            tiles = []
            for i in range(group):
                rows_q, rows_k, si = tile_rows(g * group + i)
                q = q_ref[0, rows_q, :].astype(BF16)
                zero = jnp.zeros_like(q)
                qq = jnp.concatenate([jnp.where(first_head, q, zero), jnp.where(first_head, zero, q)], axis=0)
                kw = k_ref[0, rows_k, :].astype(BF16)
                s = _dot_nt(qq, kw) + bias_ref[si]
                tiles.append((rows_q, rows_k, s))
            soft = []
            for rows_q, rows_k, s in tiles:
                mx = jnp.max(s, axis=-1, keepdims=True)
                p = jnp.exp2(s - mx)
                den = jnp.sum(p, axis=-1, keepdims=True)
                soft.append((rows_q, rows_k, p.astype(BF16), den, mx + jnp.log2(den)))
            for rows_q, rows_k, p, den, lse2 in soft:
                vw = v_ref[0, rows_k, :].astype(BF16)
                o2 = _dot(p, vw) / den
                o = jnp.where(first_head, o2[0:qb_rows], o2[qb_rows:2 * qb_rows])
                lse = jnp.where(first_head, lse2[0:qb_rows], lse2[qb_rows:2 * qb_rows])
                if not first:
                    o_prev = so_ref[rows_q, :]
                    l_prev = sl_ref[rows_q, :]
                    top = jnp.maximum(l_prev, lse)
                    w_prev = jnp.exp2(l_prev - top)
                    w_cur = jnp.exp2(lse - top)
                    tot = w_prev + w_cur
                    o = (o_prev * w_prev + o * w_cur) / tot
                    lse = top + jnp.log2(tot)
                so_ref[rows_q, :] = o
                if not last:
                    sl_ref[rows_q, :] = lse
            return carry

        n_total = d * n_blk
        grp = math.gcd(n_total, max(1, group * q_blk // qb_rows))
        lax.fori_loop(0, n_total // grp, functools.partial(group_body, group=grp), 0)

    for c in range(0, t, OUT_CHUNK):
        y_ref[0, c:c + OUT_CHUNK, :] = so_ref[c:c + OUT_CHUNK, :].astype(y_ref.dtype)


def _dilated_mixture(slopes, qkv_a, *, q_blk=128, group=8):
    b, t, _ = qkv_a.shape
    ab = W_A // LANES
    seq_blk = (1, t, LANES)
    bias_scratch = []
    for window, d in DILATED_PAIRS:
        seq, half, qb_rows, k_win = _branch_geometry(t, window, d, q_blk)
        n_shift = 1 if seq == qb_rows else 3
        bias_scratch.append(pltpu.VMEM((n_shift, 2 * qb_rows, k_win), F32))
    return pl.pallas_call(
        functools.partial(_dilated_kernel, q_blk=q_blk, group=group),
        grid=(b, ab),
        in_specs=[
            pl.BlockSpec(memory_space=pltpu.SMEM),
            pl.BlockSpec(seq_blk, lambda bi, hp: (bi, 0, hp)),
            pl.BlockSpec(seq_blk, lambda bi, hp: (bi, 0, ab + hp)),
            pl.BlockSpec(seq_blk, lambda bi, hp: (bi, 0, 2 * ab + hp)),
        ],
        out_specs=pl.BlockSpec(seq_blk, lambda bi, hp: (bi, 0, hp)),
        out_shape=jax.ShapeDtypeStruct((b, t, W_A), BF16),
        scratch_shapes=[pltpu.VMEM((t, LANES), F32), pltpu.VMEM((t, LANES), F32)] + bias_scratch,
        compiler_params=_params(("arbitrary",) * 2),
        name="dilated_mixture",
    )(slopes, qkv_a, qkv_a, qkv_a)


def _natten_kernel(q_ref, k_ref, v_ref, tbl_ref, y_ref, *, rows, group):
    n_keys = NA_ROWS * GRID_W
    lane = lax.broadcasted_iota(jnp.int32, (GRID_W, LANES), 1)
    first_head = lane < HEAD_DIM

    def group_body(g, carry):
        tiles = []
        for i in range(group):
            r = g * group + i
            rs = jnp.clip(r - NA_ROWS // 2, 0, rows - NA_ROWS)
            q0 = pl.multiple_of(r * GRID_W, GRID_W)
            k0 = pl.multiple_of(rs * GRID_W, GRID_W)
            q = q_ref[0, pl.ds(q0, GRID_W), :]
            zero = jnp.zeros_like(q)
            qq = jnp.concatenate([jnp.where(first_head, q, zero), jnp.where(first_head, zero, q)], axis=0)
            s = _dot_nt(qq, k_ref[0, pl.ds(k0, n_keys), :]) + tbl_ref[0, r - rs]
            tiles.append((q0, k0, s))
        soft = []
        for q0, k0, s in tiles:
            mx = jnp.max(s, axis=-1, keepdims=True)
            p = jnp.exp2(s - mx)
            soft.append((q0, k0, p.astype(BF16), jnp.sum(p, axis=-1, keepdims=True)))
        for q0, k0, p, den in soft:
            o2 = _dot(p, v_ref[0, pl.ds(k0, n_keys), :]) / den
            o = jnp.where(first_head, o2[0:GRID_W], o2[GRID_W:2 * GRID_W])
            y_ref[0, pl.ds(q0, GRID_W), :] = o.astype(y_ref.dtype)
        return carry

    lax.fori_loop(0, rows // group, group_body, 0)


def _natten_bias_table(rpb):
    kh = NA_ROWS
    n_h = rpb.shape[0]
    c = np.arange(GRID_W)[:, None]
    kc = np.arange(GRID_W)[None, :]
    wc = np.clip(c - NA_COLS // 2, 0, GRID_W - NA_COLS)
    col_valid = (kc >= wc) & (kc < wc + NA_COLS)
    col_idx = kc - c + NA_COLS - 1
    pick = (col_idx[None] == np.arange(2 * NA_COLS - 1)[:, None, None]) & col_valid[None]
    band = jnp.einsum("hrm,mck->hrck", rpb.astype(F32) * LOG2E, jnp.asarray(pick, F32),
                      precision=lax.Precision.HIGHEST)
    band = jnp.where(col_valid[None, None], band, NEG_BIG)
    tbl = jnp.stack([band[:, kh - 1 - dl:2 * kh - 1 - dl] for dl in range(kh)], axis=1)
    tbl = tbl.reshape(n_h // 2, 2, kh, kh, GRID_W, GRID_W)
    tbl = tbl.transpose(0, 2, 1, 4, 3, 5)
    return tbl.reshape(n_h // 2, kh, 2 * GRID_W, kh * GRID_W)


def _natten(qkv_b, rpb, *, group=16):
    b, t, _ = qkv_b.shape
    rows = t // GRID_W
    assert rows >= NA_ROWS and t % GRID_W == 0 and rows % group == 0
    tbl = _natten_bias_table(rpb)
    bb = W_B // LANES
    seq_blk = (1, t, LANES)
    return pl.pallas_call(
        functools.partial(_natten_kernel, rows=rows, group=group),
        grid=(b, bb),
        in_specs=[
            pl.BlockSpec(seq_blk, lambda bi, hp: (bi, 0, hp)),
            pl.BlockSpec(seq_blk, lambda bi, hp: (bi, 0, bb + hp)),
            pl.BlockSpec(seq_blk, lambda bi, hp: (bi, 0, 2 * bb + hp)),
            pl.BlockSpec((1, NA_ROWS, 2 * GRID_W, NA_ROWS * GRID_W), lambda bi, hp: (hp, 0, 0, 0)),
        ],
        out_specs=pl.BlockSpec(seq_blk, lambda bi, hp: (bi, 0, hp)),
        out_shape=jax.ShapeDtypeStruct((b, t, W_B), BF16),
        compiler_params=_params(("arbitrary",) * 2),
        name="natten",
    )(qkv_b, qkv_b, qkv_b, tbl)


def _diff_attn_kernel(slope_ref, q_ref, k_ref, v_ref, lq1_ref, lk1_ref, lq2_ref, lk2_ref,
                      subln_ref, y_ref, qq_ref, rel_ref, vt_ref, mx_ref,
                      m_ref, acc_ref, *s_refs, tq, tk, n_sub, lam_init):
    h = pl.program_id(1)
    seq = k_ref.shape[1]
    n_kv = seq // tk
    slope2 = slope_ref[h] * LOG2E

    key = lax.broadcasted_iota(jnp.int32, (tk, 2 * tq), 0)
    qry = lax.broadcasted_iota(jnp.int32, (tk, 2 * tq), 1)
    qry = jnp.where(qry >= tq, qry - tq, qry)
    rel = (key - qry).astype(F32) * slope2
    rel_ref[0] = rel
    rel_ref[1] = -rel
    ones_row = lax.broadcasted_iota(jnp.int32, (DEN_ROWS, tk), 0) == 0
    for c in range(n_kv):
        vt_ref[c, 0:LANES, :] = v_ref[0, c * tk:(c + 1) * tk, :].astype(F32).T.astype(BF16)
        vt_ref[c, LANES:LANES + DEN_ROWS, :] = jnp.where(ones_row, 1.0, 0.0).astype(BF16)

    lam = (jnp.exp(jnp.sum(lq1_ref[...] * lk1_ref[...], axis=-1, keepdims=True))
           - jnp.exp(jnp.sum(lq2_ref[...] * lk2_ref[...], axis=-1, keepdims=True)) + lam_init)

    def tile_fns(u, first_tile):
        i0 = pl.multiple_of((first_tile + u) * tq, tq)
        diag = i0 // tk

        def block_offset(j):
            return (j * tk - i0).astype(F32) * slope2

        def produce(j, slot, overlapping):
            j0 = pl.multiple_of(j * tk, tk)
            s = _dot_nt(k_ref[0, pl.ds(j0, tk), :], qq_ref[u])
            if overlapping:
                s = s - jnp.abs(rel_ref[0] + block_offset(j))
            else:
                s = s + rel_ref[(j > diag).astype(jnp.int32)]
            s_refs[2 * u + slot][0] = s
            mx_ref[u, slot] = jnp.max(s, axis=0, keepdims=True)

        def consume(j, slot, overlapping):
            if overlapping:
                shift = jnp.zeros((), F32)
            else:
                off = block_offset(j)
                shift = jnp.where(j > diag, -off, off)
            m_prev = m_ref[u]
            m_new = jnp.maximum(m_prev, mx_ref[u, slot] + shift)
            alpha = jnp.exp2(m_prev - m_new)
            s = s_refs[2 * u + slot][jnp.minimum(h, 0)]
            p = jnp.exp2((s - (m_new - shift)).astype(BF16))
            acc_ref[u] = alpha * acc_ref[u] + _dot(vt_ref[j], p)
            m_ref[u] = m_new

        def block_at(t):
            if t == 0:
                return diag
            return (t - 1) + ((t - 1) >= diag).astype(jnp.int32)

        def start():
            q = q_ref[0, pl.ds(i0, tq), :]
            lane = lax.broadcasted_iota(jnp.int32, (tq, LANES), 1)
            zero = jnp.zeros_like(q)
            qq_ref[u, 0:tq, :] = jnp.where(lane < HEAD_DIM, q, zero)
            qq_ref[u, tq:2 * tq, :] = jnp.where(lane < HEAD_DIM, zero, q)
            m_ref[u] = jnp.full(m_ref.shape[1:], NEG_BIG, F32)
            acc_ref[u] = jnp.zeros(acc_ref.shape[1:], F32)
            produce(block_at(0), 0, True)

        def middle():
            for t in range(n_kv):
                if t + 1 < n_kv:
                    produce(block_at(t + 1), (t + 1) % 2, False)
                consume(block_at(t), t % 2, t == 0)

        def finish():
            o = acc_ref[u, 0:LANES, :] / acc_ref[u, LANES:LANES + 1, :]
            y = (o[:, 0:tq] - lam * o[:, tq:2 * tq]).T
            y = _rms(y, subln_ref[...], SUBLN_EPS) * (1.0 - lam_init)
            y_ref[0, pl.ds(i0, tq), :] = y.astype(y_ref.dtype)

        return start, middle, finish

    def tile_group(g, carry):
        fns = [tile_fns(u, g * n_sub) for u in range(n_sub)]
        fns[0][0]()
        for u in range(n_sub):
            fns[u][1]()
            if u + 1 < n_sub:
                fns[u + 1][0]()
            fns[u][2]()
        return carry

    lax.fori_loop(0, seq // (n_sub * tq), tile_group, 0)


def _diff_attn(slopes, qkv, lq1, lk1, lq2, lk2, subln, lam_init, *, tq=256, tk=512, n_sub=8):
    b, t, n_cols = qkv.shape
    nh = n_cols // (3 * LANES)
    n_sub = math.gcd(n_sub, t // tq)
    assert t % tk == 0 and tk % tq == 0
    vec = lambda a: a.reshape(1, -1).astype(F32)
    small = lambda n: pl.BlockSpec((1, n), lambda bi, h: (0, 0))
    return pl.pallas_call(
        functools.partial(_diff_attn_kernel, tq=tq, tk=tk, n_sub=n_sub, lam_init=lam_init),
        grid=(b, nh),
        in_specs=[
            pl.BlockSpec(memory_space=pltpu.SMEM),
            pl.BlockSpec((1, t, LANES), lambda bi, h: (bi, 0, h)),
            pl.BlockSpec((1, t, LANES), lambda bi, h: (bi, 0, nh + h)),
            pl.BlockSpec((1, t, LANES), lambda bi, h: (bi, 0, 2 * nh + h)),
            small(HEAD_DIM), small(HEAD_DIM), small(HEAD_DIM), small(HEAD_DIM),
            small(2 * HEAD_DIM),
        ],
        out_specs=pl.BlockSpec((1, t, LANES), lambda bi, h: (bi, 0, h)),
        out_shape=jax.ShapeDtypeStruct((b, t, nh * LANES), BF16),
        scratch_shapes=[
            pltpu.VMEM((n_sub, 2 * tq, LANES), BF16),
            pltpu.VMEM((2, tk, 2 * tq), F32),
            pltpu.VMEM((t // tk, LANES + DEN_ROWS, tk), BF16),
            pltpu.VMEM((n_sub, 2, 1, 2 * tq), F32),
            pltpu.VMEM((n_sub, 1, 2 * tq), F32),
            pltpu.VMEM((n_sub, LANES + DEN_ROWS, 2 * tq), F32),
        ] + [pltpu.VMEM((2, tk, 2 * tq), F32)
             for _ in range(2 * n_sub)],
        compiler_params=_params(("arbitrary",) * 2),
        name="diff_attn",
    )(slopes, qkv, qkv, qkv, vec(lq1), vec(lk1), vec(lq2), vec(lk2), vec(subln))


def _alibi_slopes(n):
    return jnp.exp2(-8.0 * jnp.arange(1, n + 1, dtype=F32) / n)


def _lambda_init(layer):
    return 0.8 - 0.6 * math.exp(-0.3 * layer)


def kernel(x, a0_norm, a0_w_in, a0_w_out, a0_rpb, f0_norm, f0_w_gate, f0_w_up, f0_w_down,
           a1_norm, a1_w_qkv, a1_w_out, a1_lam_q1, a1_lam_k1, a1_lam_q2, a1_lam_k2, a1_subln,
           f1_norm, f1_w_gate, f1_w_up, f1_w_down, final_norm):
    b, t, d = x.shape
    x2 = x.reshape(b * t, d)
    qscale = HEAD_DIM ** -0.5 * LOG2E
    bf = lambda w: w.astype(BF16)

    cs0 = np.ones((3 * W_A + 3 * W_B,), np.float32)
    cs0[0:W_A] = qscale
    cs0[3 * W_A:3 * W_A + W_B] = qscale
    qkv_a, qkv_b = _norm_proj(x2, a0_norm, bf(a0_w_in), jnp.asarray(cs0),
                              ((3 * W_A, F32), (3 * W_B, BF16)))
    ya = _dilated_mixture(_alibi_slopes(N_HEADS_A), qkv_a.reshape(b, t, 3 * W_A)).reshape(b * t, W_A)
    yb = _natten(qkv_b.reshape(b, t, 3 * W_B), a0_rpb).reshape(b * t, W_B)
    x2 = _mix_ffn(x2, [ya, yb], bf(a0_w_out), f0_norm, bf(f0_w_gate), bf(f0_w_up), bf(f0_w_down))

    wc = a1_w_qkv.shape[1] // 3
    cs1 = np.ones((3 * wc,), np.float32)
    cs1[0:wc] = qscale
    (qkv1,) = _norm_proj(x2, a1_norm, bf(a1_w_qkv), jnp.asarray(cs1), ((3 * wc, BF16),))
    qkv1 = qkv1.reshape(b, t, 3 * wc)
    n_heads_c = wc // (2 * HEAD_DIM)
    y = _diff_attn(_alibi_slopes(n_heads_c), qkv1, a1_lam_q1, a1_lam_k1, a1_lam_q2, a1_lam_k2,
                   a1_subln, _lambda_init(1)).reshape(b * t, wc)
    out = _mix_ffn(x2, [y], bf(a1_w_out), f1_norm, bf(f1_w_gate), bf(f1_w_up), bf(f1_w_down),
                   final_norm)
    return out.reshape(b, t, d)
```

```python
import functools
import math

import numpy as np
import jax
import jax.numpy as jnp
from jax import lax
from jax.experimental import pallas as pl
from jax.experimental.pallas import tpu as pltpu

HEAD_DIM = 64
N_HEADS_A = 8
N_HEADS_B = 8
W_A = N_HEADS_A * HEAD_DIM
W_B = N_HEADS_B * HEAD_DIM
DILATED_PAIRS = ((128, 1), (512, 4), (2048, 16))
GRID_W = 64
NA_ROWS = 8
NA_COLS = 16
RMS_EPS = 1e-6
SUBLN_EPS = 1e-5
LOG2E = math.log2(math.e)

LANES = 128
OUT_CHUNK = 512
DEN_ROWS = 16
V7X_VMEM_LIMIT_BYTES = 56 * 1024 * 1024

NEG_BIG = -1e30

BF16 = jnp.bfloat16
F32 = jnp.float32


def _params(semantics):
    return pltpu.CompilerParams(dimension_semantics=semantics,
                                vmem_limit_bytes=V7X_VMEM_LIMIT_BYTES)


def _rms(x, g, eps):
    ms = jnp.mean(x * x, axis=-1, keepdims=True)
    return (x * lax.rsqrt(ms + eps)) * g


def _dot_nt(a, b):
    return lax.dot_general(a, b, (((1,), (1,)), ((), ())), preferred_element_type=F32)


def _dot(a, b):
    return jnp.dot(a, b, preferred_element_type=F32)


def _norm_proj_kernel(x_ref, g_ref, w_ref, cs_ref, *o_refs, n_chunk):
    h = _rms(x_ref[...], g_ref[...], RMS_EPS).astype(BF16)
    col = 0
    for o_ref in o_refs:
        for c in range(0, o_ref.shape[1], n_chunk):
            acc = _dot(h, w_ref[:, col + c:col + c + n_chunk])
            o_ref[:, c:c + n_chunk] = (acc * cs_ref[:, col + c:col + c + n_chunk]).astype(o_ref.dtype)
        col += o_ref.shape[1]


def _norm_proj(x2, g, w, colscale, outs, *, tm=1024, n_chunk=512):
    m, d = x2.shape
    n = w.shape[1]
    assert sum(c for c, _ in outs) == n and all(c % n_chunk == 0 for c, _ in outs)
    return pl.pallas_call(
        functools.partial(_norm_proj_kernel, n_chunk=n_chunk),
        grid=(m // tm,),
        in_specs=[
            pl.BlockSpec((tm, d), lambda i: (i, 0)),
            pl.BlockSpec((1, d), lambda i: (0, 0)),
            pl.BlockSpec((d, n), lambda i: (0, 0)),
            pl.BlockSpec((1, n), lambda i: (0, 0)),
        ],
        out_specs=[pl.BlockSpec((tm, c), lambda i: (i, 0)) for c, _ in outs],
        out_shape=[jax.ShapeDtypeStruct((m, c), dt) for c, dt in outs],
        compiler_params=_params(("arbitrary",)),
        name="norm_proj",
    )(x2, g.reshape(1, d), w, colscale.reshape(1, n))


def _mix_ffn_kernel(*refs, n_parts, ff_chunk, final):
    x_ref = refs[0]
    part_refs = refs[1:1 + n_parts]
    wo_ref, g_ref, wg_ref, wu_ref, wd_ref = refs[1 + n_parts:6 + n_parts]
    fg_ref = refs[6 + n_parts] if final else None
    o_ref = refs[-1]

    y = jnp.concatenate([p_ref[...] for p_ref in part_refs], axis=1)
    x = x_ref[...] + _dot(y, wo_ref[...])
    h = _rms(x, g_ref[...], RMS_EPS).astype(BF16)
    d_ff = wg_ref.shape[1]
    acc = x
    for c in range(0, d_ff, ff_chunk):
        gate = _dot(h, wg_ref[:, c:c + ff_chunk])
        up = _dot(h, wu_ref[:, c:c + ff_chunk])
        a = (gate * jax.nn.sigmoid(gate) * up).astype(BF16)
        acc = acc + _dot(a, wd_ref[c:c + ff_chunk, :])
    if final:
        acc = _rms(acc, fg_ref[...], RMS_EPS)
    o_ref[...] = acc


def _mix_ffn(x2, parts, w_out, g, wg, wu, wd, final_g=None, *, tm=1024, ff_chunk=256):
    m, d = x2.shape
    d_ff = wg.shape[1]
    final = final_g is not None
    const = lambda i: (0, 0)
    in_specs = [pl.BlockSpec((tm, d), lambda i: (i, 0))]
    in_specs += [pl.BlockSpec((tm, p.shape[1]), lambda i: (i, 0)) for p in parts]
    in_specs += [
        pl.BlockSpec((d, d), const),
        pl.BlockSpec((1, d), const),
        pl.BlockSpec((d, d_ff), const),
        pl.BlockSpec((d, d_ff), const),
        pl.BlockSpec((d_ff, d), const),
    ]
    args = [x2, *parts, w_out, g.reshape(1, d), wg, wu, wd]
    if final:
        in_specs.append(pl.BlockSpec((1, d), const))
        args.append(final_g.reshape(1, d))
    return pl.pallas_call(
        functools.partial(_mix_ffn_kernel, n_parts=len(parts), ff_chunk=ff_chunk, final=final),
        grid=(m // tm,),
        in_specs=in_specs,
        out_specs=pl.BlockSpec((tm, d), lambda i: (i, 0)),
        out_shape=jax.ShapeDtypeStruct((m, d), F32),
        compiler_params=_params(("arbitrary",)),
        name="mix_ffn",
    )(*args)


def _branch_geometry(t, window, dilation, q_blk):
    seq = t // dilation
    half = window // (2 * dilation)
    if seq <= 2 * q_blk:
        q_blk = seq
    k_win = min(q_blk + 2 * half, seq)
    assert seq % q_blk == 0 and q_blk % HEAD_DIM == 0
    assert q_blk == 2 * half or seq == q_blk
    return seq, half, q_blk, k_win


def _dilated_kernel(slope_ref, q_ref, k_ref, v_ref, y_ref, so_ref, sl_ref, *bias_refs, q_blk, group):
    hp = pl.program_id(1)
    t = q_ref.shape[1]
    order = sorted(range(len(DILATED_PAIRS)), key=lambda i: DILATED_PAIRS[i][1])

    for step, bi in enumerate(order):
        window, d = DILATED_PAIRS[bi]
        seq, half, qb_rows, k_win = _branch_geometry(t, window, d, q_blk)
        bias_ref = bias_refs[bi]
        first, last = step == 0, step == len(order) - 1

        row = lax.broadcasted_iota(jnp.int32, (2 * qb_rows, k_win), 0)
        col = lax.broadcasted_iota(jnp.int32, (2 * qb_rows, k_win), 1)
        second = row >= qb_rows
        rel = col - jnp.where(second, row - qb_rows, row)
        slope2 = jnp.where(second, slope_ref[2 * hp + 1], slope_ref[2 * hp]) * (LOG2E * d)
        for si in range(bias_ref.shape[0]):
            dist = jnp.abs(rel - si * half)
            bias_ref[si] = jnp.where(dist <= half, -slope2 * dist.astype(F32), NEG_BIG)

        lane = lax.broadcasted_iota(jnp.int32, (qb_rows, LANES), 1)
        first_head = lane < HEAD_DIM
        n_blk = seq // qb_rows

        def tile_rows(idx, d=d, half=half, seq=seq, qb_rows=qb_rows, k_win=k_win, n_blk=n_blk):
            r = lax.div(idx, jnp.int32(n_blk))
            q0 = lax.rem(idx, jnp.int32(n_blk)) * qb_rows
            ws = jnp.clip(q0 - half, 0, seq - k_win)
            if d == 1:
                rows_q = pl.ds(pl.multiple_of(q0, qb_rows), qb_rows)
                rows_k = pl.ds(pl.multiple_of(ws, HEAD_DIM), k_win)
            else:
                rows_q = pl.ds(r + d * q0, qb_rows, stride=d)
                rows_k = pl.ds(r + d * ws, k_win, stride=d)
            return rows_q, rows_k, lax.div(q0 - ws, jnp.int32(half))

        def group_body(g, carry, group, d=d, qb_rows=qb_rows, bias_ref=bias_ref, first=first, last=last,
                       first_head=first_head, tile_rows=tile_rows):
            tiles = []
            for i in range(group):
                rows_q, rows_k, si = tile_rows(g * group + i)
                q = q_ref[0, rows_q, :].astype(BF16)
                zero = jnp.zeros_like(q)
                qq = jnp.concatenate([jnp.where(first_head, q, zero), jnp.where(first_head, zero, q)], axis=0)
                kw = k_ref[0, rows_k, :].astype(BF16)
                s = _dot_nt(qq, kw) + bias_ref[si]
                tiles.append((rows_q, rows_k, s))
            soft = []
            for rows_q, rows_k, s in tiles:
                mx = jnp.max(s, axis=-1, keepdims=True)
                p = jnp.exp2(s - mx)
                den = jnp.sum(p, axis=-1, keepdims=True)
                soft.append((rows_q, rows_k, p.astype(BF16), den, mx + jnp.log2(den)))
            for rows_q, rows_k, p, den, lse2 in soft:
                vw = v_ref[0, rows_k, :].astype(BF16)
                o2 = _dot(p, vw) / den
                o = jnp.where(first_head, o2[0:qb_rows], o2[qb_rows:2 * qb_rows])
                lse = jnp.where(first_head, lse2[0:qb_rows], lse2[qb_rows:2 * qb_rows])
                if not first:
                    o_prev = so_ref[rows_q, :]
                    l_prev = sl_ref[rows_q, :]
                    top = jnp.maximum(l_prev, lse)
                    w_prev = jnp.exp2(l_prev - top)
                    w_cur = jnp.exp2(lse - top)
                    tot = w_prev + w_cur
                    o = (o_prev * w_prev + o * w_cur) / tot
                    lse = top + jnp.log2(tot)
                so_ref[rows_q, :] = o
                if not last:
                    sl_ref[rows_q, :] = lse
            return carry

        n_total = d * n_blk
        grp = math.gcd(n_total, max(1, group * q_blk // qb_rows))
        lax.fori_loop(0, n_total // grp, functools.partial(group_body, group=grp), 0)

    for c in range(0, t, OUT_CHUNK):
        y_ref[0, c:c + OUT_CHUNK, :] = so_ref[c:c + OUT_CHUNK, :].astype(y_ref.dtype)


def _dilated_mixture(slopes, qkv_a, *, q_blk=128, group=8):
    b, t, _ = qkv_a.shape
    ab = W_A // LANES
    seq_blk = (1, t, LANES)
    bias_scratch = []
    for window, d in DILATED_PAIRS:
        seq, half, qb_rows, k_win = _branch_geometry(t, window, d, q_blk)
        n_shift = 1 if seq == qb_rows else 3
        bias_scratch.append(pltpu.VMEM((n_shift, 2 * qb_rows, k_win), F32))
    return pl.pallas_call(
        functools.partial(_dilated_kernel, q_blk=q_blk, group=group),
        grid=(b, ab),
        in_specs=[
            pl.BlockSpec(memory_space=pltpu.SMEM),
            pl.BlockSpec(seq_blk, lambda bi, hp: (bi, 0, hp)),
            pl.BlockSpec(seq_blk, lambda bi, hp: (bi, 0, ab + hp)),
            pl.BlockSpec(seq_blk, lambda bi, hp: (bi, 0, 2 * ab + hp)),
        ],
        out_specs=pl.BlockSpec(seq_blk, lambda bi, hp: (bi, 0, hp)),
        out_shape=jax.ShapeDtypeStruct((b, t, W_A), BF16),
        scratch_shapes=[pltpu.VMEM((t, LANES), F32), pltpu.VMEM((t, LANES), F32)] + bias_scratch,
        compiler_params=_params(("arbitrary",) * 2),
        name="dilated_mixture",
    )(slopes, qkv_a, qkv_a, qkv_a)


def _natten_kernel(q_ref, k_ref, v_ref, tbl_ref, y_ref, *, rows, group):
    n_keys = NA_ROWS * GRID_W
    lane = lax.broadcasted_iota(jnp.int32, (GRID_W, LANES), 1)
    first_head = lane < HEAD_DIM

    def group_body(g, carry):
        tiles = []
        for i in range(group):
            r = g * group + i
            rs = jnp.clip(r - NA_ROWS // 2, 0, rows - NA_ROWS)
            q0 = pl.multiple_of(r * GRID_W, GRID_W)
            k0 = pl.multiple_of(rs * GRID_W, GRID_W)
            q = q_ref[0, pl.ds(q0, GRID_W), :]
            zero = jnp.zeros_like(q)
            qq = jnp.concatenate([jnp.where(first_head, q, zero), jnp.where(first_head, zero, q)], axis=0)
            s = _dot_nt(qq, k_ref[0, pl.ds(k0, n_keys), :]) + tbl_ref[0, r - rs]
            tiles.append((q0, k0, s))
        soft = []
        for q0, k0, s in tiles:
            mx = jnp.max(s, axis=-1, keepdims=True)
            p = jnp.exp2(s - mx)
            soft.append((q0, k0, p.astype(BF16), jnp.sum(p, axis=-1, keepdims=True)))
        for q0, k0, p, den in soft:
            o2 = _dot(p, v_ref[0, pl.ds(k0, n_keys), :]) / den
            o = jnp.where(first_head, o2[0:GRID_W], o2[GRID_W:2 * GRID_W])
            y_ref[0, pl.ds(q0, GRID_W), :] = o.astype(y_ref.dtype)
        return carry

    lax.fori_loop(0, rows // group, group_body, 0)


def _natten_bias_table(rpb):
    kh = NA_ROWS
    n_h = rpb.shape[0]
    c = np.arange(GRID_W)[:, None]
    kc = np.arange(GRID_W)[None, :]
    wc = np.clip(c - NA_COLS // 2, 0, GRID_W - NA_COLS)
    col_valid = (kc >= wc) & (kc < wc + NA_COLS)
    col_idx = kc - c + NA_COLS - 1
    pick = (col_idx[None] == np.arange(2 * NA_COLS - 1)[:, None, None]) & col_valid[None]
    band = jnp.einsum("hrm,mck->hrck", rpb.astype(F32) * LOG2E, jnp.asarray(pick, F32),
                      precision=lax.Precision.HIGHEST)
    band = jnp.where(col_valid[None, None], band, NEG_BIG)
    tbl = jnp.stack([band[:, kh - 1 - dl:2 * kh - 1 - dl] for dl in range(kh)], axis=1)
    tbl = tbl.reshape(n_h // 2, 2, kh, kh, GRID_W, GRID_W)
    tbl = tbl.transpose(0, 2, 1, 4, 3, 5)
    return tbl.reshape(n_h // 2, kh, 2 * GRID_W, kh * GRID_W)


def _natten(qkv_b, rpb, *, group=16):
    b, t, _ = qkv_b.shape
    rows = t // GRID_W
    assert rows >= NA_ROWS and t % GRID_W == 0 and rows % group == 0
    tbl = _natten_bias_table(rpb)
    bb = W_B // LANES
    seq_blk = (1, t, LANES)
    return pl.pallas_call(
        functools.partial(_natten_kernel, rows=rows, group=group),
        grid=(b, bb),
        in_specs=[
            pl.BlockSpec(seq_blk, lambda bi, hp: (bi, 0, hp)),
            pl.BlockSpec(seq_blk, lambda bi, hp: (bi, 0, bb + hp)),
            pl.BlockSpec(seq_blk, lambda bi, hp: (bi, 0, 2 * bb + hp)),
            pl.BlockSpec((1, NA_ROWS, 2 * GRID_W, NA_ROWS * GRID_W), lambda bi, hp: (hp, 0, 0, 0)),
        ],
        out_specs=pl.BlockSpec(seq_blk, lambda bi, hp: (bi, 0, hp)),
        out_shape=jax.ShapeDtypeStruct((b, t, W_B), BF16),
        compiler_params=_params(("arbitrary",) * 2),
        name="natten",
    )(qkv_b, qkv_b, qkv_b, tbl)


def _diff_attn_kernel(slope_ref, q_ref, k_ref, v_ref, lq1_ref, lk1_ref, lq2_ref, lk2_ref,
                      subln_ref, y_ref, qq_ref, rel_ref, vt_ref, mx_ref,
                      m_ref, acc_ref, *s_refs, tq, tk, n_sub, lam_init):
    h = pl.program_id(1)
    seq = k_ref.shape[1]
    n_kv = seq // tk
    slope2 = slope_ref[h] * LOG2E

    key = lax.broadcasted_iota(jnp.int32, (tk, 2 * tq), 0)
    qry = lax.broadcasted_iota(jnp.int32, (tk, 2 * tq), 1)
    qry = jnp.where(qry >= tq, qry - tq, qry)
    rel = (key - qry).astype(F32) * slope2
    rel_ref[0] = rel
    rel_ref[1] = -rel
    ones_row = lax.broadcasted_iota(jnp.int32, (DEN_ROWS, tk), 0) == 0
    for c in range(n_kv):
        vt_ref[c, 0:LANES, :] = v_ref[0, c * tk:(c + 1) * tk, :].astype(F32).T.astype(BF16)
        vt_ref[c, LANES:LANES + DEN_ROWS, :] = jnp.where(ones_row, 1.0, 0.0).astype(BF16)

    lam = (jnp.exp(jnp.sum(lq1_ref[...] * lk1_ref[...], axis=-1, keepdims=True))
           - jnp.exp(jnp.sum(lq2_ref[...] * lk2_ref[...], axis=-1, keepdims=True)) + lam_init)

    AHEAD = 2
    SLOTS = AHEAD + 1
    half = n_sub // 2

    def tile_fns(u, first_tile):
        i0 = pl.multiple_of((first_tile + u) * tq, tq)
        diag = i0 // tk

        def block_offset(j):
            return (j * tk - i0).astype(F32) * slope2

        def produce(j, slot, overlapping):
            j0 = pl.multiple_of(j * tk, tk)
            s = _dot_nt(k_ref[0, pl.ds(j0, tk), :], qq_ref[u])
            if overlapping:
                s = s - jnp.abs(rel_ref[0] + block_offset(j))
            else:
                s = s + rel_ref[(j > diag).astype(jnp.int32)]
            s_refs[SLOTS * (u % half) + slot][u // half] = s
            mx_ref[u, slot] = jnp.max(s, axis=0, keepdims=True)

        def consume(j, slot, overlapping):
            if overlapping:
                shift = jnp.zeros((), F32)
            else:
                off = block_offset(j)
                shift = jnp.where(j > diag, -off, off)
            m_prev = m_ref[u]
            m_new = jnp.maximum(m_prev, mx_ref[u, slot] + shift)
            alpha = jnp.exp2(m_prev - m_new)
            s = s_refs[SLOTS * (u % half) + slot][u // half + jnp.minimum(h, 0)]
            p = jnp.exp2((s - (m_new - shift)).astype(BF16))
            acc_ref[u] = alpha * acc_ref[u] + _dot(vt_ref[j], p)
            m_ref[u] = m_new

        def block_at(t):
            if t == 0:
                return diag
            return (t - 1) + ((t - 1) >= diag).astype(jnp.int32)

        def start():
            q = q_ref[0, pl.ds(i0, tq), :]
            lane = lax.broadcasted_iota(jnp.int32, (tq, LANES), 1)
            zero = jnp.zeros_like(q)
            qq_ref[u, 0:tq, :] = jnp.where(lane < HEAD_DIM, q, zero)
            qq_ref[u, tq:2 * tq, :] = jnp.where(lane < HEAD_DIM, zero, q)
            m_ref[u] = jnp.full(m_ref.shape[1:], NEG_BIG, F32)
            acc_ref[u] = jnp.zeros(acc_ref.shape[1:], F32)
            produce(block_at(0), 0, True)
            for t in range(1, min(AHEAD, n_kv)):
                produce(block_at(t), t % SLOTS, False)

        def middle():
            for t in range(n_kv):
                if t + AHEAD < n_kv:
                    produce(block_at(t + AHEAD), (t + AHEAD) % SLOTS, False)
                consume(block_at(t), t % SLOTS, t == 0)

        def finish():
            o = acc_ref[u, 0:LANES, :] / acc_ref[u, LANES:LANES + 1, :]
            y = (o[:, 0:tq] - lam * o[:, tq:2 * tq]).T
            y = _rms(y, subln_ref[...], SUBLN_EPS) * (1.0 - lam_init)
            y_ref[0, pl.ds(i0, tq), :] = y.astype(y_ref.dtype)

        return start, middle, finish

    def tile_group(g, carry):
        fns = [tile_fns(u, g * n_sub) for u in range(n_sub)]
        fns[0][0]()
        for u in range(n_sub):
            fns[u][1]()
            if u + 1 < n_sub:
                fns[u + 1][0]()
            fns[u][2]()
        return carry

    lax.fori_loop(0, seq // (n_sub * tq), tile_group, 0)


def _diff_attn(slopes, qkv, lq1, lk1, lq2, lk2, subln, lam_init, *, tq=256, tk=512, n_sub=8):
    b, t, n_cols = qkv.shape
    nh = n_cols // (3 * LANES)
    n_sub = math.gcd(n_sub, t // tq)
    assert t % tk == 0 and tk % tq == 0 and n_sub % 2 == 0
    vec = lambda a: a.reshape(1, -1).astype(F32)
    small = lambda n: pl.BlockSpec((1, n), lambda bi, h: (0, 0))
    return pl.pallas_call(
        functools.partial(_diff_attn_kernel, tq=tq, tk=tk, n_sub=n_sub, lam_init=lam_init),
        grid=(b, nh),
        in_specs=[
            pl.BlockSpec(memory_space=pltpu.SMEM),
            pl.BlockSpec((1, t, LANES), lambda bi, h: (bi, 0, h)),
            pl.BlockSpec((1, t, LANES), lambda bi, h: (bi, 0, nh + h)),
            pl.BlockSpec((1, t, LANES), lambda bi, h: (bi, 0, 2 * nh + h)),
            small(HEAD_DIM), small(HEAD_DIM), small(HEAD_DIM), small(HEAD_DIM),
            small(2 * HEAD_DIM),
        ],
        out_specs=pl.BlockSpec((1, t, LANES), lambda bi, h: (bi, 0, h)),
        out_shape=jax.ShapeDtypeStruct((b, t, nh * LANES), BF16),
        scratch_shapes=[
            pltpu.VMEM((n_sub, 2 * tq, LANES), BF16),
            pltpu.VMEM((2, tk, 2 * tq), F32),
            pltpu.VMEM((t // tk, LANES + DEN_ROWS, tk), BF16),
            pltpu.VMEM((n_sub, 3, 1, 2 * tq), F32),
            pltpu.VMEM((n_sub, 1, 2 * tq), F32),
            pltpu.VMEM((n_sub, LANES + DEN_ROWS, 2 * tq), F32),
        ] + [pltpu.VMEM((2, tk, 2 * tq), F32)
             for _ in range(3 * n_sub // 2)],
        compiler_params=_params(("arbitrary",) * 2),
        name="diff_attn",
    )(slopes, qkv, qkv, qkv, vec(lq1), vec(lk1), vec(lq2), vec(lk2), vec(subln))


def _alibi_slopes(n):
    return jnp.exp2(-8.0 * jnp.arange(1, n + 1, dtype=F32) / n)


def _lambda_init(layer):
    return 0.8 - 0.6 * math.exp(-0.3 * layer)


def kernel(x, a0_norm, a0_w_in, a0_w_out, a0_rpb, f0_norm, f0_w_gate, f0_w_up, f0_w_down,
           a1_norm, a1_w_qkv, a1_w_out, a1_lam_q1, a1_lam_k1, a1_lam_q2, a1_lam_k2, a1_subln,
           f1_norm, f1_w_gate, f1_w_up, f1_w_down, final_norm):
    b, t, d = x.shape
    x2 = x.reshape(b * t, d)
    qscale = HEAD_DIM ** -0.5 * LOG2E
    bf = lambda w: w.astype(BF16)

    cs0 = np.ones((3 * W_A + 3 * W_B,), np.float32)
    cs0[0:W_A] = qscale
    cs0[3 * W_A:3 * W_A + W_B] = qscale
    qkv_a, qkv_b = _norm_proj(x2, a0_norm, bf(a0_w_in), jnp.asarray(cs0),
                              ((3 * W_A, F32), (3 * W_B, BF16)))
    ya = _dilated_mixture(_alibi_slopes(N_HEADS_A), qkv_a.reshape(b, t, 3 * W_A)).reshape(b * t, W_A)
    yb = _natten(qkv_b.reshape(b, t, 3 * W_B), a0_rpb).reshape(b * t, W_B)
    x2 = _mix_ffn(x2, [ya, yb], bf(a0_w_out), f0_norm, bf(f0_w_gate), bf(f0_w_up), bf(f0_w_down))

    wc = a1_w_qkv.shape[1] // 3
    cs1 = np.ones((3 * wc,), np.float32)
    cs1[0:wc] = qscale
    (qkv1,) = _norm_proj(x2, a1_norm, bf(a1_w_qkv), jnp.asarray(cs1), ((3 * wc, BF16),))
    qkv1 = qkv1.reshape(b, t, 3 * wc)
    n_heads_c = wc // (2 * HEAD_DIM)
    y = _diff_attn(_alibi_slopes(n_heads_c), qkv1, a1_lam_q1, a1_lam_k1, a1_lam_q2, a1_lam_k2,
                   a1_subln, _lambda_init(1)).reshape(b * t, wc)
    out = _mix_ffn(x2, [y], bf(a1_w_out), f1_norm, bf(f1_w_gate), bf(f1_w_up), bf(f1_w_down),
                   final_norm)
    return out.reshape(b, t, d)
```

```python
import functools
import math

import numpy as np
import jax
import jax.numpy as jnp
from jax import lax
from jax.experimental import pallas as pl
from jax.experimental.pallas import tpu as pltpu

HEAD_DIM = 64
N_HEADS_A = 8
N_HEADS_B = 8
W_A = N_HEADS_A * HEAD_DIM
W_B = N_HEADS_B * HEAD_DIM
DILATED_PAIRS = ((128, 1), (512, 4), (2048, 16))
GRID_W = 64
NA_ROWS = 8
NA_COLS = 16
RMS_EPS = 1e-6
SUBLN_EPS = 1e-5
LOG2E = math.log2(math.e)

LANES = 128
OUT_CHUNK = 512
DEN_ROWS = 16
V7X_VMEM_LIMIT_BYTES = 56 * 1024 * 1024

NEG_BIG = -1e30

BF16 = jnp.bfloat16
F32 = jnp.float32


def _params(semantics):
    return pltpu.CompilerParams(dimension_semantics=semantics,
                                vmem_limit_bytes=V7X_VMEM_LIMIT_BYTES)


def _rms(x, g, eps):
    ms = jnp.mean(x * x, axis=-1, keepdims=True)
    return (x * lax.rsqrt(ms + eps)) * g


def _dot_nt(a, b):
    return lax.dot_general(a, b, (((1,), (1,)), ((), ())), preferred_element_type=F32)


def _dot(a, b):
    return jnp.dot(a, b, preferred_element_type=F32)


def _norm_proj_kernel(x_ref, g_ref, w_ref, cs_ref, *o_refs, n_chunk):
    h = _rms(x_ref[...], g_ref[...], RMS_EPS).astype(BF16)
    col = 0
    for o_ref in o_refs:
        for c in range(0, o_ref.shape[1], n_chunk):
            acc = _dot(h, w_ref[:, col + c:col + c + n_chunk])
            o_ref[:, c:c + n_chunk] = (acc * cs_ref[:, col + c:col + c + n_chunk]).astype(o_ref.dtype)
        col += o_ref.shape[1]


def _norm_proj(x2, g, w, colscale, outs, *, tm=1024, n_chunk=512):
    m, d = x2.shape
    n = w.shape[1]
    assert sum(c for c, _ in outs) == n and all(c % n_chunk == 0 for c, _ in outs)
    return pl.pallas_call(
        functools.partial(_norm_proj_kernel, n_chunk=n_chunk),
        grid=(m // tm,),
        in_specs=[
            pl.BlockSpec((tm, d), lambda i: (i, 0)),
            pl.BlockSpec((1, d), lambda i: (0, 0)),
            pl.BlockSpec((d, n), lambda i: (0, 0)),
            pl.BlockSpec((1, n), lambda i: (0, 0)),
        ],
        out_specs=[pl.BlockSpec((tm, c), lambda i: (i, 0)) for c, _ in outs],
        out_shape=[jax.ShapeDtypeStruct((m, c), dt) for c, dt in outs],
        compiler_params=_params(("arbitrary",)),
        name="norm_proj",
    )(x2, g.reshape(1, d), w, colscale.reshape(1, n))


def _mix_ffn_kernel(*refs, n_parts, ff_chunk, final):
    x_ref = refs[0]
    part_refs = refs[1:1 + n_parts]
    wo_ref, g_ref, wg_ref, wu_ref, wd_ref = refs[1 + n_parts:6 + n_parts]
    fg_ref = refs[6 + n_parts] if final else None
    o_ref = refs[-1]

    y = jnp.concatenate([p_ref[...] for p_ref in part_refs], axis=1)
    x = x_ref[...] + _dot(y, wo_ref[...])
    h = _rms(x, g_ref[...], RMS_EPS).astype(BF16)
    d_ff = wg_ref.shape[1]
    acc = x
    for c in range(0, d_ff, ff_chunk):
        gate = _dot(h, wg_ref[:, c:c + ff_chunk])
        up = _dot(h, wu_ref[:, c:c + ff_chunk])
        a = (gate * jax.nn.sigmoid(gate) * up).astype(BF16)
        acc = acc + _dot(a, wd_ref[c:c + ff_chunk, :])
    if final:
        acc = _rms(acc, fg_ref[...], RMS_EPS)
    o_ref[...] = acc


def _mix_ffn(x2, parts, w_out, g, wg, wu, wd, final_g=None, *, tm=1024, ff_chunk=256):
    m, d = x2.shape
    d_ff = wg.shape[1]
    final = final_g is not None
    const = lambda i: (0, 0)
    in_specs = [pl.BlockSpec((tm, d), lambda i: (i, 0))]
    in_specs += [pl.BlockSpec((tm, p.shape[1]), lambda i: (i, 0)) for p in parts]
    in_specs += [
        pl.BlockSpec((d, d), const),
        pl.BlockSpec((1, d), const),
        pl.BlockSpec((d, d_ff), const),
        pl.BlockSpec((d, d_ff), const),
        pl.BlockSpec((d_ff, d), const),
    ]
    args = [x2, *parts, w_out, g.reshape(1, d), wg, wu, wd]
    if final:
        in_specs.append(pl.BlockSpec((1, d), const))
        args.append(final_g.reshape(1, d))
    return pl.pallas_call(
        functools.partial(_mix_ffn_kernel, n_parts=len(parts), ff_chunk=ff_chunk, final=final),
        grid=(m // tm,),
        in_specs=in_specs,
        out_specs=pl.BlockSpec((tm, d), lambda i: (i, 0)),
        out_shape=jax.ShapeDtypeStruct((m, d), F32),
        compiler_params=_params(("arbitrary",)),
        name="mix_ffn",
    )(*args)


def _branch_geometry(t, window, dilation, q_blk):
    seq = t // dilation
    half = window // (2 * dilation)
    if seq <= 2 * q_blk:
        q_blk = seq
    k_win = min(q_blk + 2 * half, seq)
    assert seq % q_blk == 0 and q_blk % HEAD_DIM == 0
    assert q_blk == 2 * half or seq == q_blk
    return seq, half, q_blk, k_win


def _dilated_kernel(slope_ref, q_ref, k_ref, v_ref, y_ref, so_ref, sl_ref, *bias_refs, q_blk, group):
    hp = pl.program_id(1)
    t = q_ref.shape[1]
    order = sorted(range(len(DILATED_PAIRS)), key=lambda i: DILATED_PAIRS[i][1])

    for step, bi in enumerate(order):
        window, d = DILATED_PAIRS[bi]
        seq, half, qb_rows, k_win = _branch_geometry(t, window, d, q_blk)
        bias_ref = bias_refs[bi]
        first, last = step == 0, step == len(order) - 1

        row = lax.broadcasted_iota(jnp.int32, (2 * qb_rows, k_win), 0)
        col = lax.broadcasted_iota(jnp.int32, (2 * qb_rows, k_win), 1)
        second = row >= qb_rows
        rel = col - jnp.where(second, row - qb_rows, row)
        slope2 = jnp.where(second, slope_ref[2 * hp + 1], slope_ref[2 * hp]) * (LOG2E * d)
        for si in range(bias_ref.shape[0]):
            dist = jnp.abs(rel - si * half)
            bias_ref[si] = jnp.where(dist <= half, -slope2 * dist.astype(F32), NEG_BIG)

        lane = lax.broadcasted_iota(jnp.int32, (qb_rows, LANES), 1)
        first_head = lane < HEAD_DIM
        n_blk = seq // qb_rows

        def tile_rows(idx, d=d, half=half, seq=seq, qb_rows=qb_rows, k_win=k_win, n_blk=n_blk):
            r = lax.div(idx, jnp.int32(n_blk))
            q0 = lax.rem(idx, jnp.int32(n_blk)) * qb_rows
            ws = jnp.clip(q0 - half, 0, seq - k_win)
            if d == 1:
                rows_q = pl.ds(pl.multiple_of(q0, qb_rows), qb_rows)
                rows_k = pl.ds(pl.multiple_of(ws, HEAD_DIM), k_win)
            else:
                rows_q = pl.ds(r + d * q0, qb_rows, stride=d)
                rows_k = pl.ds(r + d * ws, k_win, stride=d)
            return rows_q, rows_k, lax.div(q0 - ws, jnp.int32(half))

        def group_body(g, carry, group, d=d, qb_rows=qb_rows, bias_ref=bias_ref, first=first, last=last,
                       first_head=first_head, tile_rows=tile_rows):
            tiles = []
            for i in range(group):
                rows_q, rows_k, si = tile_rows(g * group + i)
                q = q_ref[0, rows_q, :].astype(BF16)
                zero = jnp.zeros_like(q)
                qq = jnp.concatenate([jnp.where(first_head, q, zero), jnp.where(first_head, zero, q)], axis=0)
                kw = k_ref[0, rows_k, :].astype(BF16)
                s = _dot_nt(qq, kw) + bias_ref[si]
                tiles.append((rows_q, rows_k, s))
            soft = []
            for rows_q, rows_k, s in tiles:
                mx = jnp.max(s, axis=-1, keepdims=True)
                p = jnp.exp2(s - mx)
                den = jnp.sum(p, axis=-1, keepdims=True)
                soft.append((rows_q, rows_k, p.astype(BF16), den, mx + jnp.log2(den)))
            for rows_q, rows_k, p, den, lse2 in soft:
                vw = v_ref[0, rows_k, :].astype(BF16)
                o2 = _dot(p, vw) / den
                o = jnp.where(first_head, o2[0:qb_rows], o2[qb_rows:2 * qb_rows])
                lse = jnp.where(first_head, lse2[0:qb_rows], lse2[qb_rows:2 * qb_rows])
                if not first:
                    o_prev = so_ref[rows_q, :]
                    l_prev = sl_ref[rows_q, :]
                    top = jnp.maximum(l_prev, lse)
                    w_prev = jnp.exp2(l_prev - top)
                    w_cur = jnp.exp2(lse - top)
                    tot = w_prev + w_cur
                    o = (o_prev * w_prev + o * w_cur) / tot
                    lse = top + jnp.log2(tot)
                so_ref[rows_q, :] = o
                if not last:
                    sl_ref[rows_q, :] = lse
            return carry

        n_total = d * n_blk
        grp = math.gcd(n_total, max(1, group * q_blk // qb_rows))
        lax.fori_loop(0, n_total // grp, functools.partial(group_body, group=grp), 0)

    for c in range(0, t, OUT_CHUNK):
        y_ref[0, c:c + OUT_CHUNK, :] = so_ref[c:c + OUT_CHUNK, :].astype(y_ref.dtype)


def _dilated_mixture(slopes, qkv_a, *, q_blk=128, group=8):
    b, t, _ = qkv_a.shape
    ab = W_A // LANES
    seq_blk = (1, t, LANES)
    bias_scratch = []
    for window, d in DILATED_PAIRS:
        seq, half, qb_rows, k_win = _branch_geometry(t, window, d, q_blk)
        n_shift = 1 if seq == qb_rows else 3
        bias_scratch.append(pltpu.VMEM((n_shift, 2 * qb_rows, k_win), F32))
    return pl.pallas_call(
        functools.partial(_dilated_kernel, q_blk=q_blk, group=group),
        grid=(b, ab),
        in_specs=[
            pl.BlockSpec(memory_space=pltpu.SMEM),
            pl.BlockSpec(seq_blk, lambda bi, hp: (bi, 0, hp)),
            pl.BlockSpec(seq_blk, lambda bi, hp: (bi, 0, ab + hp)),
            pl.BlockSpec(seq_blk, lambda bi, hp: (bi, 0, 2 * ab + hp)),
        ],
        out_specs=pl.BlockSpec(seq_blk, lambda bi, hp: (bi, 0, hp)),
        out_shape=jax.ShapeDtypeStruct((b, t, W_A), BF16),
        scratch_shapes=[pltpu.VMEM((t, LANES), F32), pltpu.VMEM((t, LANES), F32)] + bias_scratch,
        compiler_params=_params(("arbitrary",) * 2),
        name="dilated_mixture",
    )(slopes, qkv_a, qkv_a, qkv_a)


def _natten_kernel(q_ref, k_ref, v_ref, tbl_ref, y_ref, *, rows, group):
    n_keys = NA_ROWS * GRID_W
    lane = lax.broadcasted_iota(jnp.int32, (GRID_W, LANES), 1)
    first_head = lane < HEAD_DIM

    def group_body(g, carry):
        tiles = []
        for i in range(group):
            r = g * group + i
            rs = jnp.clip(r - NA_ROWS // 2, 0, rows - NA_ROWS)
            q0 = pl.multiple_of(r * GRID_W, GRID_W)
            k0 = pl.multiple_of(rs * GRID_W, GRID_W)
            q = q_ref[0, pl.ds(q0, GRID_W), :]
            zero = jnp.zeros_like(q)
            qq = jnp.concatenate([jnp.where(first_head, q, zero), jnp.where(first_head, zero, q)], axis=0)
            s = _dot_nt(qq, k_ref[0, pl.ds(k0, n_keys), :]) + tbl_ref[0, r - rs]
            tiles.append((q0, k0, s))
        soft = []
        for q0, k0, s in tiles:
            mx = jnp.max(s, axis=-1, keepdims=True)
            p = jnp.exp2(s - mx)
            soft.append((q0, k0, p.astype(BF16), jnp.sum(p, axis=-1, keepdims=True)))
        for q0, k0, p, den in soft:
            o2 = _dot(p, v_ref[0, pl.ds(k0, n_keys), :]) / den
            o = jnp.where(first_head, o2[0:GRID_W], o2[GRID_W:2 * GRID_W])
            y_ref[0, pl.ds(q0, GRID_W), :] = o.astype(y_ref.dtype)
        return carry

    lax.fori_loop(0, rows // group, group_body, 0)


def _natten_bias_table(rpb):
    kh = NA_ROWS
    n_h = rpb.shape[0]
    c = np.arange(GRID_W)[:, None]
    kc = np.arange(GRID_W)[None, :]
    wc = np.clip(c - NA_COLS // 2, 0, GRID_W - NA_COLS)
    col_valid = (kc >= wc) & (kc < wc + NA_COLS)
    col_idx = kc - c + NA_COLS - 1
    pick = (col_idx[None] == np.arange(2 * NA_COLS - 1)[:, None, None]) & col_valid[None]
    band = jnp.einsum("hrm,mck->hrck", rpb.astype(F32) * LOG2E, jnp.asarray(pick, F32),
                      precision=lax.Precision.HIGHEST)
    band = jnp.where(col_valid[None, None], band, NEG_BIG)
    tbl = jnp.stack([band[:, kh - 1 - dl:2 * kh - 1 - dl] for dl in range(kh)], axis=1)
    tbl = tbl.reshape(n_h // 2, 2, kh, kh, GRID_W, GRID_W)
    tbl = tbl.transpose(0, 2, 1, 4, 3, 5)
    return tbl.reshape(n_h // 2, kh, 2 * GRID_W, kh * GRID_W)


def _natten(qkv_b, rpb, *, group=16):
    b, t, _ = qkv_b.shape
    rows = t // GRID_W
    assert rows >= NA_ROWS and t % GRID_W == 0 and rows % group == 0
    tbl = _natten_bias_table(rpb)
    bb = W_B // LANES
    seq_blk = (1, t, LANES)
    return pl.pallas_call(
        functools.partial(_natten_kernel, rows=rows, group=group),
        grid=(b, bb),
        in_specs=[
            pl.BlockSpec(seq_blk, lambda bi, hp: (bi, 0, hp)),
            pl.BlockSpec(seq_blk, lambda bi, hp: (bi, 0, bb + hp)),
            pl.BlockSpec(seq_blk, lambda bi, hp: (bi, 0, 2 * bb + hp)),
            pl.BlockSpec((1, NA_ROWS, 2 * GRID_W, NA_ROWS * GRID_W), lambda bi, hp: (hp, 0, 0, 0)),
        ],
        out_specs=pl.BlockSpec(seq_blk, lambda bi, hp: (bi, 0, hp)),
        out_shape=jax.ShapeDtypeStruct((b, t, W_B), BF16),
        compiler_params=_params(("arbitrary",) * 2),
        name="natten",
    )(qkv_b, qkv_b, qkv_b, tbl)


def _diff_attn_kernel(slope_ref, q_ref, k_ref, v_ref, lq1_ref, lk1_ref, lq2_ref, lk2_ref,
                      subln_ref, y_ref, qq_ref, rel_ref, vt_ref, mx_ref,
                      m_ref, acc_ref, *s_refs, tq, tk, n_sub, lam_init):
    h = pl.program_id(1)
    seq = k_ref.shape[1]
    n_kv = seq // tk
    slope2 = slope_ref[h] * LOG2E

    key = lax.broadcasted_iota(jnp.int32, (tk, 2 * tq), 0)
    qry = lax.broadcasted_iota(jnp.int32, (tk, 2 * tq), 1)
    qry = jnp.where(qry >= tq, qry - tq, qry)
    rel = (key - qry).astype(F32) * slope2
    rel_ref[0] = rel
    rel_ref[1] = -rel
    ones_row = lax.broadcasted_iota(jnp.int32, (DEN_ROWS, tk), 0) == 0
    for c in range(n_kv):
        vt_ref[c, 0:LANES, :] = v_ref[0, c * tk:(c + 1) * tk, :].astype(F32).T.astype(BF16)
        vt_ref[c, LANES:LANES + DEN_ROWS, :] = jnp.where(ones_row, 1.0, 0.0).astype(BF16)

    lam = (jnp.exp(jnp.sum(lq1_ref[...] * lk1_ref[...], axis=-1, keepdims=True))
           - jnp.exp(jnp.sum(lq2_ref[...] * lk2_ref[...], axis=-1, keepdims=True)) + lam_init)

    AHEAD = 3
    SLOTS = AHEAD + 1
    half = n_sub // 2

    def tile_fns(u, first_tile):
        i0 = pl.multiple_of((first_tile + u) * tq, tq)
        diag = i0 // tk

        def block_offset(j):
            return (j * tk - i0).astype(F32) * slope2

        def produce(j, slot, overlapping):
            j0 = pl.multiple_of(j * tk, tk)
            s = _dot_nt(k_ref[0, pl.ds(j0, tk), :], qq_ref[u])
            if overlapping:
                s = s - jnp.abs(rel_ref[0] + block_offset(j))
            else:
                s = s + rel_ref[(j > diag).astype(jnp.int32)]
            s_refs[SLOTS * (u % half) + slot][u // half] = s
            mx_ref[u, slot] = jnp.max(s, axis=0, keepdims=True)

        def consume(j, slot, overlapping):
            if overlapping:
                shift = jnp.zeros((), F32)
            else:
                off = block_offset(j)
                shift = jnp.where(j > diag, -off, off)
            m_prev = m_ref[u]
            m_new = jnp.maximum(m_prev, mx_ref[u, slot] + shift)
            alpha = jnp.exp2(m_prev - m_new)
            s = s_refs[SLOTS * (u % half) + slot][u // half + jnp.minimum(h, 0)]
            p = jnp.exp2((s - (m_new - shift)).astype(BF16))
            acc_ref[u] = alpha * acc_ref[u] + _dot(vt_ref[j], p)
            m_ref[u] = m_new

        def block_at(t):
            if t == 0:
                return diag
            return (t - 1) + ((t - 1) >= diag).astype(jnp.int32)

        def start():
            q = q_ref[0, pl.ds(i0, tq), :]
            lane = lax.broadcasted_iota(jnp.int32, (tq, LANES), 1)
            zero = jnp.zeros_like(q)
            qq_ref[u, 0:tq, :] = jnp.where(lane < HEAD_DIM, q, zero)
            qq_ref[u, tq:2 * tq, :] = jnp.where(lane < HEAD_DIM, zero, q)
            m_ref[u] = jnp.full(m_ref.shape[1:], NEG_BIG, F32)
            acc_ref[u] = jnp.zeros(acc_ref.shape[1:], F32)
            produce(block_at(0), 0, True)
            for t in range(1, min(AHEAD, n_kv)):
                produce(block_at(t), t % SLOTS, False)

        def middle():
            for t in range(n_kv):
                if t + AHEAD < n_kv:
                    produce(block_at(t + AHEAD), (t + AHEAD) % SLOTS, False)
                consume(block_at(t), t % SLOTS, t == 0)

        def finish():
            o = acc_ref[u, 0:LANES, :] / acc_ref[u, LANES:LANES + 1, :]
            y = (o[:, 0:tq] - lam * o[:, tq:2 * tq]).T
            y = _rms(y, subln_ref[...], SUBLN_EPS) * (1.0 - lam_init)
            y_ref[0, pl.ds(i0, tq), :] = y.astype(y_ref.dtype)

        return start, middle, finish

    def tile_group(g, carry):
        fns = [tile_fns(u, g * n_sub) for u in range(n_sub)]
        fns[0][0]()
        for u in range(n_sub):
            fns[u][1]()
            if u + 1 < n_sub:
                fns[u + 1][0]()
            fns[u][2]()
        return carry

    lax.fori_loop(0, seq // (n_sub * tq), tile_group, 0)


def _diff_attn(slopes, qkv, lq1, lk1, lq2, lk2, subln, lam_init, *, tq=256, tk=512, n_sub=8):
    b, t, n_cols = qkv.shape
    nh = n_cols // (3 * LANES)
    n_sub = math.gcd(n_sub, t // tq)
    assert t % tk == 0 and tk % tq == 0 and n_sub % 2 == 0
    vec = lambda a: a.reshape(1, -1).astype(F32)
    small = lambda n: pl.BlockSpec((1, n), lambda bi, h: (0, 0))
    return pl.pallas_call(
        functools.partial(_diff_attn_kernel, tq=tq, tk=tk, n_sub=n_sub, lam_init=lam_init),
        grid=(b, nh),
        in_specs=[
            pl.BlockSpec(memory_space=pltpu.SMEM),
            pl.BlockSpec((1, t, LANES), lambda bi, h: (bi, 0, h)),
            pl.BlockSpec((1, t, LANES), lambda bi, h: (bi, 0, nh + h)),
            pl.BlockSpec((1, t, LANES), lambda bi, h: (bi, 0, 2 * nh + h)),
            small(HEAD_DIM), small(HEAD_DIM), small(HEAD_DIM), small(HEAD_DIM),
            small(2 * HEAD_DIM),
        ],
        out_specs=pl.BlockSpec((1, t, LANES), lambda bi, h: (bi, 0, h)),
        out_shape=jax.ShapeDtypeStruct((b, t, nh * LANES), BF16),
        scratch_shapes=[
            pltpu.VMEM((n_sub, 2 * tq, LANES), BF16),
            pltpu.VMEM((2, tk, 2 * tq), F32),
            pltpu.VMEM((t // tk, LANES + DEN_ROWS, tk), BF16),
            pltpu.VMEM((n_sub, 4, 1, 2 * tq), F32),
            pltpu.VMEM((n_sub, 1, 2 * tq), F32),
            pltpu.VMEM((n_sub, LANES + DEN_ROWS, 2 * tq), F32),
        ] + [pltpu.VMEM((2, tk, 2 * tq), F32)
             for _ in range(4 * n_sub // 2)],
        compiler_params=_params(("arbitrary",) * 2),
        name="diff_attn",
    )(slopes, qkv, qkv, qkv, vec(lq1), vec(lk1), vec(lq2), vec(lk2), vec(subln))


def _alibi_slopes(n):
    return jnp.exp2(-8.0 * jnp.arange(1, n + 1, dtype=F32) / n)


def _lambda_init(layer):
    return 0.8 - 0.6 * math.exp(-0.3 * layer)


def kernel(x, a0_norm, a0_w_in, a0_w_out, a0_rpb, f0_norm, f0_w_gate, f0_w_up, f0_w_down,
           a1_norm, a1_w_qkv, a1_w_out, a1_lam_q1, a1_lam_k1, a1_lam_q2, a1_lam_k2, a1_subln,
           f1_norm, f1_w_gate, f1_w_up, f1_w_down, final_norm):
    b, t, d = x.shape
    x2 = x.reshape(b * t, d)
    qscale = HEAD_DIM ** -0.5 * LOG2E
    bf = lambda w: w.astype(BF16)

    cs0 = np.ones((3 * W_A + 3 * W_B,), np.float32)
    cs0[0:W_A] = qscale
    cs0[3 * W_A:3 * W_A + W_B] = qscale
    qkv_a, qkv_b = _norm_proj(x2, a0_norm, bf(a0_w_in), jnp.asarray(cs0),
                              ((3 * W_A, F32), (3 * W_B, BF16)))
    ya = _dilated_mixture(_alibi_slopes(N_HEADS_A), qkv_a.reshape(b, t, 3 * W_A)).reshape(b * t, W_A)
    yb = _natten(qkv_b.reshape(b, t, 3 * W_B), a0_rpb).reshape(b * t, W_B)
    x2 = _mix_ffn(x2, [ya, yb], bf(a0_w_out), f0_norm, bf(f0_w_gate), bf(f0_w_up), bf(f0_w_down))

    wc = a1_w_qkv.shape[1] // 3
    cs1 = np.ones((3 * wc,), np.float32)
    cs1[0:wc] = qscale
    (qkv1,) = _norm_proj(x2, a1_norm, bf(a1_w_qkv), jnp.asarray(cs1), ((3 * wc, BF16),))
    qkv1 = qkv1.reshape(b, t, 3 * wc)
    n_heads_c = wc // (2 * HEAD_DIM)
    y = _diff_attn(_alibi_slopes(n_heads_c), qkv1, a1_lam_q1, a1_lam_k1, a1_lam_q2, a1_lam_k2,
                   a1_subln, _lambda_init(1)).reshape(b * t, wc)
    out = _mix_ffn(x2, [y], bf(a1_w_out), f1_norm, bf(f1_w_gate), bf(f1_w_up), bf(f1_w_down),
                   final_norm)
    return out.reshape(b, t, d)
```
